```python
import jax, jax.numpy as jnp
from jax import lax
import numpy as np

D_MODEL = 4096
BATCH = 1
SEQ = 8192
DEPTH = 1

CHUNK = 64
EPS = 1e-6

CONV_MIX_WIDTH = D_MODEL // 2
CONV_GROUPS = 16
CONV_A_WIDTH = 3

DN_HEAD_DIM = 128
DN_HEADS = (D_MODEL - CONV_MIX_WIDTH) // DN_HEAD_DIM
DN_WIDTH = DN_HEADS * DN_HEAD_DIM
DN_CONV_WIDTH = 4

MIX_WIDTH = CONV_MIX_WIDTH + DN_WIDTH
IN_PROJ_COLS = 3 * CONV_MIX_WIDTH + 4 * DN_WIDTH + 2 * DN_HEADS

N_EXPERTS = 32
TOP_K = 4
EXPERT_FF = D_MODEL // 2
SWIGLU_LIMIT = 7.0
SWIGLU_ALPHA = 1.702
EXPERT_BLOCK = 128

kernel_name = "hybrid_shortconv_gdn_moe_block"


def rmsnorm(x, w):
    xf = x.astype(jnp.float32)
    y = xf * lax.rsqrt(jnp.mean(xf * xf, axis=-1, keepdims=True) + EPS)
    return (y * w.astype(jnp.float32)).astype(x.dtype)


def l2norm(x):
    xf = x.astype(jnp.float32)
    return xf * lax.rsqrt(jnp.sum(xf * xf, axis=-1, keepdims=True) + EPS)


def causal_depthwise_conv(x, w):
    K = w.shape[0]
    S = x.shape[1]
    xp = jnp.pad(x, ((0, 0), (K - 1, 0), (0, 0)))
    return sum(xp[:, k:k + S] * w[k] for k in range(K))


def gated_delta_rule(q, k, v, g, beta):
    Bsz, S, H, dk = q.shape
    dv = v.shape[-1]
    N = S // CHUNK
    q = q.astype(jnp.float32) * (dk ** -0.5)
    k = k.astype(jnp.float32)
    v = v.astype(jnp.float32)

    def chunks(t):
        return jnp.swapaxes(t.reshape(Bsz, N, CHUNK, H, t.shape[-1]), 2, 3)

    q, k, v = chunks(q), chunks(k), chunks(v)
    gc = jnp.cumsum(jnp.swapaxes(g.astype(jnp.float32).reshape(Bsz, N, CHUNK, H), 2, 3), axis=-1)
    beta = jnp.swapaxes(beta.astype(jnp.float32).reshape(Bsz, N, CHUNK, H), 2, 3)[..., None]

    idx = jnp.arange(CHUNK)
    incl = idx[:, None] >= idx[None, :]
    strict = idx[:, None] > idx[None, :]
    decay = jnp.exp(jnp.where(incl, gc[..., :, None] - gc[..., None, :], -jnp.inf))

    kb = k * beta
    vb = v * beta
    L = jnp.where(strict, jnp.einsum('bnhid,bnhjd->bnhij', kb, k) * decay, 0.0)
    A = L + jnp.eye(CHUNK, dtype=jnp.float32)
    u = lax.linalg.triangular_solve(A, vb, left_side=True, lower=True, unit_diagonal=True)
    w = lax.linalg.triangular_solve(A, kb * jnp.exp(gc)[..., None], left_side=True, lower=True,
                                    unit_diagonal=True)
    qk = jnp.einsum('bnhid,bnhjd->bnhij', q, k) * decay
    q_dec = q * jnp.exp(gc)[..., None]
    g_last = gc[..., -1]
    k_tail = k * jnp.exp(g_last[..., None] - gc)[..., None]

    def step(state, inp):
        q_i, qk_i, w_i, u_i, kt_i, gl_i = inp
        v_new = u_i - jnp.einsum('bhcd,bhde->bhce', w_i, state)
        o = jnp.einsum('bhcd,bhde->bhce', q_i, state) + jnp.einsum('bhij,bhje->bhie', qk_i, v_new)
        state = state * jnp.exp(gl_i)[..., None, None] + jnp.einsum('bhcd,bhce->bhde', kt_i, v_new)
        return state, o

    xs = tuple(jnp.moveaxis(t, 1, 0) for t in (q_dec, qk, w, u, k_tail, g_last))
    state0 = jnp.zeros((Bsz, H, dk, dv), jnp.float32)
    _, o = lax.scan(step, state0, xs)
    o = jnp.transpose(o, (1, 0, 3, 2, 4))
    return o.reshape(Bsz, S, H, dv)


def token_mix(xn, w_in, conv_a_w, conv_a_norm_w, dn_conv_w, dn_A_log, dn_dt_bias, dn_norm_w, w_out):
    Bsz, S, _ = xn.shape
    proj = xn @ w_in
    bounds = [CONV_MIX_WIDTH, CONV_MIX_WIDTH, CONV_MIX_WIDTH, 3 * DN_WIDTH, DN_WIDTH, DN_HEADS]
    splits = tuple(int(s) for s in np.cumsum(bounds))
    x_in, b_gate, c_gate, qkv, z, a, b = jnp.split(proj, splits, axis=-1)

    y_a = b_gate * causal_depthwise_conv(c_gate * x_in, conv_a_w)
    y_a = rmsnorm(y_a.reshape(Bsz, S, CONV_GROUPS, -1), conv_a_norm_w.reshape(CONV_GROUPS, -1))
    y_a = y_a.reshape(Bsz, S, CONV_MIX_WIDTH)

    qkv = jax.nn.silu(causal_depthwise_conv(qkv, dn_conv_w))
    q, k, v = jnp.split(qkv, 3, axis=-1)
    q = l2norm(q.reshape(Bsz, S, DN_HEADS, DN_HEAD_DIM))
    k = l2norm(k.reshape(Bsz, S, DN_HEADS, DN_HEAD_DIM))
    v = v.reshape(Bsz, S, DN_HEADS, DN_HEAD_DIM)
    beta = jax.nn.sigmoid(b.astype(jnp.float32))
    g = -jnp.exp(dn_A_log.astype(jnp.float32)) * jax.nn.softplus(a.astype(jnp.float32) + dn_dt_bias.astype(jnp.float32))
    o = gated_delta_rule(q, k, v, g, beta)
    o = rmsnorm(o, dn_norm_w) * jax.nn.silu(z.astype(jnp.float32).reshape(Bsz, S, DN_HEADS, DN_HEAD_DIM))
    y_b = o.reshape(Bsz, S, DN_WIDTH).astype(xn.dtype)

    return jnp.concatenate([y_a, y_b], axis=-1) @ w_out


def moe(xn, l, w_router, b_router, w_gate_up, b_gate_up, w_down, b_down):
    Bsz, S, D = xn.shape
    T = Bsz * S
    xf = xn.reshape(T, D)
    logits = (xf @ w_router[l] + b_router[l]).astype(jnp.float32)
    top_val, top_idx = lax.top_k(logits, TOP_K)
    gates = jax.nn.softmax(top_val, axis=-1)

    n_pairs = T * TOP_K
    e_flat = top_idx.reshape(-1)
    tok_flat = jnp.arange(n_pairs, dtype=jnp.int32) // TOP_K
    order = jnp.argsort(e_flat)
    e_sorted = e_flat[order]
    counts = jnp.zeros((N_EXPERTS,), jnp.int32).at[e_flat].add(1)
    padded = ((counts + EXPERT_BLOCK - 1) // EXPERT_BLOCK) * EXPERT_BLOCK
    start = jnp.cumsum(counts) - counts
    pend = jnp.cumsum(padded)
    pstart = pend - padded
    dest = pstart[e_sorted] + (jnp.arange(n_pairs, dtype=jnp.int32) - start[e_sorted])
    n_slots = n_pairs + N_EXPERTS * EXPERT_BLOCK
    n_blocks = n_slots // EXPERT_BLOCK
    slot_tok = jnp.full((n_slots,), T, jnp.int32).at[dest].set(tok_flat[order])
    slot_gate = jnp.zeros((n_slots,), jnp.float32).at[dest].set(gates.reshape(-1)[order])
    block_expert = jnp.clip(jnp.searchsorted(pend, jnp.arange(n_blocks, dtype=jnp.int32) * EXPERT_BLOCK,
                                             side='right'), 0, N_EXPERTS - 1)

    x_pad = jnp.concatenate([xf, jnp.zeros((1, D), xf.dtype)], axis=0)
    xs = x_pad[slot_tok].reshape(n_blocks, EXPERT_BLOCK, D)

    def expert_block(args):
        xb, e = args
        h = xb @ w_gate_up[l, e] + b_gate_up[l, e]
        gate, up = h[:, :EXPERT_FF], h[:, EXPERT_FF:]
        gate = jnp.minimum(gate, SWIGLU_LIMIT)
        up = jnp.clip(up, -SWIGLU_LIMIT, SWIGLU_LIMIT)
        glu = gate * jax.nn.sigmoid(SWIGLU_ALPHA * gate)
        return ((up + 1.0) * glu) @ w_down[l, e] + b_down[l, e]

    ys = lax.map(expert_block, (xs, block_expert)).reshape(n_slots, D)
    ys = ys * slot_gate[:, None].astype(ys.dtype)
    y = jax.ops.segment_sum(ys, slot_tok, num_segments=T + 1)[:T]
    return y.reshape(Bsz, S, D)


def setup_inputs(seed: int = 0) -> dict:
    key = jax.random.key(seed)
    ks = jax.random.split(key, 20)
    f32 = jnp.float32
    x = jax.random.normal(ks[0], (BATCH, SEQ, D_MODEL), f32)
    norm1_w = 1.0 + 0.02 * jax.random.normal(ks[1], (DEPTH, D_MODEL), f32)
    w_in = jax.random.normal(ks[2], (DEPTH, D_MODEL, IN_PROJ_COLS), f32) * D_MODEL ** -0.5
    conv_a_w = jax.random.normal(ks[3], (DEPTH, CONV_A_WIDTH, CONV_MIX_WIDTH), f32) * CONV_A_WIDTH ** -0.5
    conv_a_norm_w = 1.0 + 0.02 * jax.random.normal(ks[4], (DEPTH, CONV_MIX_WIDTH), f32)
    dn_conv_w = jax.random.normal(ks[5], (DEPTH, DN_CONV_WIDTH, 3 * DN_WIDTH), f32) * DN_CONV_WIDTH ** -0.5
    dn_A_log = jnp.log(jax.random.uniform(ks[6], (DEPTH, DN_HEADS), f32, 1.0, 16.0))
    dt = jnp.exp(jax.random.uniform(ks[7], (DEPTH, DN_HEADS), f32, float(np.log(1e-3)), float(np.log(1e-1))))
    dn_dt_bias = dt + jnp.log(-jnp.expm1(-dt))
    dn_norm_w = 1.0 + 0.02 * jax.random.normal(ks[8], (DEPTH, DN_HEAD_DIM), f32)
    w_out = jax.random.normal(ks[9], (DEPTH, MIX_WIDTH, D_MODEL), f32) * MIX_WIDTH ** -0.5
    norm2_w = 1.0 + 0.02 * jax.random.normal(ks[10], (DEPTH, D_MODEL), f32)
    w_router = jax.random.normal(ks[11], (DEPTH, D_MODEL, N_EXPERTS), f32) * D_MODEL ** -0.5
    b_router = 0.01 * jax.random.normal(ks[12], (DEPTH, N_EXPERTS), f32)
    w_gate_up = jax.random.normal(ks[13], (DEPTH, N_EXPERTS, D_MODEL, 2 * EXPERT_FF), f32) * D_MODEL ** -0.5
    b_gate_up = 0.02 * jax.random.normal(ks[14], (DEPTH, N_EXPERTS, 2 * EXPERT_FF), f32)
    w_down = jax.random.normal(ks[15], (DEPTH, N_EXPERTS, EXPERT_FF, D_MODEL), f32) * EXPERT_FF ** -0.5
    b_down = 0.02 * jax.random.normal(ks[16], (DEPTH, N_EXPERTS, D_MODEL), f32)
    final_norm_w = 1.0 + 0.02 * jax.random.normal(ks[17], (D_MODEL,), f32)
    return {"x": x, "norm1_w": norm1_w, "w_in": w_in, "conv_a_w": conv_a_w, "conv_a_norm_w": conv_a_norm_w,
            "dn_conv_w": dn_conv_w, "dn_A_log": dn_A_log, "dn_dt_bias": dn_dt_bias, "dn_norm_w": dn_norm_w,
            "w_out": w_out, "norm2_w": norm2_w, "w_router": w_router, "b_router": b_router,
            "w_gate_up": w_gate_up, "b_gate_up": b_gate_up, "w_down": w_down, "b_down": b_down,
            "final_norm_w": final_norm_w}


def reference(x, norm1_w, w_in, conv_a_w, conv_a_norm_w, dn_conv_w, dn_A_log, dn_dt_bias, dn_norm_w,
              w_out, norm2_w, w_router, b_router, w_gate_up, b_gate_up, w_down, b_down, final_norm_w):
    h = x
    for l in range(DEPTH):
        h = h + token_mix(rmsnorm(h, norm1_w[l]), w_in[l], conv_a_w[l], conv_a_norm_w[l], dn_conv_w[l],
                          dn_A_log[l], dn_dt_bias[l], dn_norm_w[l], w_out[l])
        h = h + moe(rmsnorm(h, norm2_w[l]), l, w_router, b_router, w_gate_up, b_gate_up, w_down, b_down)
    return rmsnorm(h, final_norm_w)
```

```python
import functools

import jax
import jax.numpy as jnp
from jax import lax
from jax.experimental import pallas as pl
from jax.experimental.pallas import tpu as pltpu

F32 = jnp.float32
BF16 = jnp.bfloat16
I32 = jnp.int32

EPS = 1e-6
CHUNK = 64
HEAD_DIM = 128
N_HEADS = 16
CONV_W = 2048
DN_W = 2048
N_EXPERTS = 32
TOP_K = 4
EXPERT_FF = 2048
SWIGLU_LIMIT = 7.0
SWIGLU_ALPHA = 1.702

LANE = 128
VMEM_LIMIT = 56 * 1024 * 1024
SLOT_BLK = 256


def _cparams(sem):
    return pltpu.CompilerParams(dimension_semantics=sem, vmem_limit_bytes=VMEM_LIMIT)


def _mm(a, b):
    return jnp.dot(a.astype(BF16), b.astype(BF16), preferred_element_type=F32)


def _mm_nt(a, b):
    return lax.dot_general(a.astype(BF16), b.astype(BF16), (((1,), (1,)), ((), ())),
                           preferred_element_type=F32)


def _mm_tn(a, b):
    return lax.dot_general(a.astype(BF16), b.astype(BF16), (((0,), (0,)), ((), ())),
                           preferred_element_type=F32)


def _mm3(x, w):
    xh = x.astype(BF16)
    xl = (x - xh.astype(F32)).astype(BF16)
    wh = w.astype(BF16)
    wl = (w - wh.astype(F32)).astype(BF16)
    return (jnp.dot(xh, wh, preferred_element_type=F32)
            + jnp.dot(xl, wh, preferred_element_type=F32)
            + jnp.dot(xh, wl, preferred_element_type=F32))


def _sigmoid(x):
    return 1.0 / (1.0 + jnp.exp(-x))


def _norm1_kernel(x_ref, nw_ref, wab_ref, alog_ref, dtb_ref, xn_ref, gc_ref, beta_ref):
    x = x_ref[...]
    ms = jnp.mean(x * x, axis=-1, keepdims=True)
    xn = x * lax.rsqrt(ms + EPS) * nw_ref[...]
    xn_ref[...] = xn.astype(BF16)
    ab = _mm3(xn, wab_ref[...])
    a = ab[:, :N_HEADS] + dtb_ref[...]
    b = ab[:, N_HEADS:]
    softplus = jnp.maximum(a, 0.0) + jnp.log1p(jnp.exp(-jnp.abs(a)))
    g = -jnp.exp(alog_ref[...]) * softplus
    pos = lax.broadcasted_iota(I32, g.shape, 0) % CHUNK
    shift = 1
    while shift < CHUNK:
        g = g + jnp.where(pos >= shift, pltpu.roll(g, shift, 0), 0.0)
        shift *= 2
    gc_ref[...] = g
    beta_ref[...] = _sigmoid(b)


def _norm1(x, nw, wab, alog, dtb, bm=256):
    T, D = x.shape
    return pl.pallas_call(
        _norm1_kernel,
        grid=(T // bm,),
        in_specs=[pl.BlockSpec((bm, D), lambda i: (i, 0)),
                  pl.BlockSpec((1, D), lambda i: (0, 0)),
                  pl.BlockSpec((D, 2 * N_HEADS), lambda i: (0, 0)),
                  pl.BlockSpec((1, N_HEADS), lambda i: (0, 0)),
                  pl.BlockSpec((1, N_HEADS), lambda i: (0, 0))],
        out_specs=[pl.BlockSpec((bm, D), lambda i: (i, 0)),
                   pl.BlockSpec((bm, N_HEADS), lambda i: (i, 0)),
                   pl.BlockSpec((bm, N_HEADS), lambda i: (i, 0))],
        out_shape=[jax.ShapeDtypeStruct((T, D), BF16),
                   jax.ShapeDtypeStruct((T, N_HEADS), F32),
                   jax.ShapeDtypeStruct((T, N_HEADS), F32)],
        compiler_params=_cparams(("arbitrary",)),
        name="norm1",
    )(x, nw, wab, alog, dtb)


def _in_proj_kernel(x_ref, w_ref, o_ref, wb_ref):
    @pl.when(pl.program_id(1) == 0)
    def _():
        wb_ref[...] = w_ref[...].astype(BF16)

    o_ref[...] = jnp.dot(x_ref[...], wb_ref[...], preferred_element_type=F32)


def _in_proj(xn, w_in, n_cols, bm=1024, bn=512):
    T, D = xn.shape
    return pl.pallas_call(
        _in_proj_kernel,
        grid=(n_cols // bn, T // bm),
        in_specs=[pl.BlockSpec((bm, D), lambda j, i: (i, 0)),
                  pl.BlockSpec((None, D, bn), lambda j, i: (0, 0, j))],
        out_specs=pl.BlockSpec((bm, bn), lambda j, i: (i, j)),
        out_shape=jax.ShapeDtypeStruct((T, n_cols), F32),
        scratch_shapes=[pltpu.VMEM((D, bn), BF16)],
        compiler_params=_cparams(("arbitrary", "arbitrary")),
        name="in_proj",
    )(xn, w_in)


def _conv_mix_kernel(x_ref, b_ref, c_ref, xh_ref, ch_ref, cw_ref, nw_ref, o_ref):
    bm, bc = x_ref.shape
    u = c_ref[...] * x_ref[...]
    halo = jnp.where(pl.program_id(0) > 0, ch_ref[...] * xh_ref[...], 0.0)
    ext = jnp.concatenate([halo, u], axis=0)
    cw = cw_ref[...]
    conv = ext[6:6 + bm] * cw[0:1] + ext[7:7 + bm] * cw[1:2] + u * cw[2:3]
    y = b_ref[...] * conv
    nw = nw_ref[...]
    for g in range(bc // LANE):
        cs = slice(g * LANE, (g + 1) * LANE)
        yg = y[:, cs]
        ms = jnp.mean(yg * yg, axis=-1, keepdims=True)
        o_ref[:, cs] = (yg * lax.rsqrt(ms + EPS) * nw[:, cs]).astype(BF16)


def _conv_mix(proj, conv_w, norm_w, bm=512, bc=512):
    T = proj.shape[0]
    nc = CONV_W // bc
    hb = bm // 8
    halo_map = lambda off: (lambda i, j: (jnp.maximum(i * hb - 1, 0), j + off))
    return pl.pallas_call(
        _conv_mix_kernel,
        grid=(T // bm, nc),
        in_specs=[pl.BlockSpec((bm, bc), lambda i, j: (i, j)),
                  pl.BlockSpec((bm, bc), lambda i, j: (i, j + nc)),
                  pl.BlockSpec((bm, bc), lambda i, j: (i, j + 2 * nc)),
                  pl.BlockSpec((8, bc), halo_map(0)),
                  pl.BlockSpec((8, bc), halo_map(2 * nc)),
                  pl.BlockSpec((3, bc), lambda i, j: (0, j)),
                  pl.BlockSpec((1, bc), lambda i, j: (0, j))],
        out_specs=pl.BlockSpec((bm, bc), lambda i, j: (i, j)),
        out_shape=jax.ShapeDtypeStruct((T, CONV_W), BF16),
        compiler_params=_cparams(("arbitrary", "arbitrary")),
        name="conv_mix",
    )(proj, proj, proj, proj, proj, conv_w, norm_w)


def _gdn_kernel(q_ref, k_ref, v_ref, z_ref, gc_ref, beta_ref, gct_ref, cw_ref, nw_ref,
                y_ref, qs_ref, ks_ref, vs_ref, halo_ref, s_ref):
    tb = q_ref.shape[0]
    t = pl.program_id(0)

    @pl.when(t == 0)
    def _():
        halo_ref[...] = jnp.zeros(halo_ref.shape, F32)
        s_ref[...] = jnp.zeros(s_ref.shape, F32)

    for a, (src, dst) in enumerate(((q_ref, qs_ref), (k_ref, ks_ref), (v_ref, vs_ref))):
        for h in range(N_HEADS):
            cs = slice(h * HEAD_DIM, (h + 1) * HEAD_DIM)
            raw = src[:, cs]
            ext = jnp.concatenate([halo_ref[a, :, cs], raw], axis=0)
            cw = cw_ref[:, a * DN_W + h * HEAD_DIM:a * DN_W + (h + 1) * HEAD_DIM]
            c = (ext[5:5 + tb] * cw[0:1] + ext[6:6 + tb] * cw[1:2]
                 + ext[7:7 + tb] * cw[2:3] + raw * cw[3:4])
            c = c * _sigmoid(c)
            if a < 2:
                c = c * lax.rsqrt(jnp.sum(c * c, axis=-1, keepdims=True) + EPS)
            if a == 0:
                c = c * (HEAD_DIM ** -0.5)
            dst[:, cs] = c
            halo_ref[a, :, cs] = raw[tb - 8:tb]

    ii = lax.broadcasted_iota(I32, (CHUNK, CHUNK), 0)
    jj = lax.broadcasted_iota(I32, (CHUNK, CHUNK), 1)
    eye = (ii == jj).astype(F32)
    nw = nw_ref[...]

    def chunk(c, carry):
        r0 = pl.multiple_of(c * CHUNK, CHUNK)
        cg = t * (tb // CHUNK) + c
        gcb = gc_ref[pl.ds(r0, CHUNK), :]
        btb = beta_ref[pl.ds(r0, CHUNK), :]
        for h in range(N_HEADS):
            cs = slice(h * HEAD_DIM, (h + 1) * HEAD_DIM)
            q = qs_ref[pl.ds(r0, CHUNK), cs]
            k = ks_ref[pl.ds(r0, CHUNK), cs]
            v = vs_ref[pl.ds(r0, CHUNK), cs]
            gcol = gcb[:, h:h + 1]
            bcol = btb[:, h:h + 1]
            grow = gct_ref[h, pl.ds(cg, 1), :]
            eg = jnp.exp(gcol)
            kb = k * bcol
            a1 = _mm_nt(jnp.concatenate([kb, q], axis=0), k)
            dec = jnp.where(ii >= jj, jnp.exp(jnp.minimum(gcol - grow, 0.0)), 0.0)
            lmat = jnp.where(ii > jj, a1[:CHUNK] * dec, 0.0)
            qkd = a1[CHUNK:] * dec
            p = eye - lmat
            cur = _mm(lmat, lmat)
            power = 2
            while 2 * power < CHUNK:
                r = _mm(jnp.concatenate([p, cur], axis=0), cur)
                p = p + r[:CHUNK]
                cur = r[CHUNK:]
                power *= 2
            p = p + _mm(p, cur)
            uw = _mm(p, jnp.concatenate([v * bcol, kb * eg], axis=1))
            u = uw[:, :HEAD_DIM]
            w = uw[:, HEAD_DIM:]
            s = s_ref[h]
            ws = _mm(jnp.concatenate([w, q * eg], axis=0), s)
            vn = u - ws[:CHUNK]
            o = ws[CHUNK:] + _mm(qkd, vn)
            gl = grow[:, CHUNK - 1:CHUNK]
            kt = k * jnp.exp(gl - gcol)
            s_ref[h] = s * jnp.exp(gl) + _mm_tn(kt, vn)
            on = o * lax.rsqrt(jnp.mean(o * o, axis=-1, keepdims=True) + EPS) * nw
            zz = z_ref[pl.ds(r0, CHUNK), cs]
            y_ref[pl.ds(r0, CHUNK), cs] = (on * (zz * _sigmoid(zz))).astype(BF16)
        return carry

    lax.fori_loop(0, tb // CHUNK, chunk, 0)


def _gdn(proj, gc, beta, gct, conv_w, norm_w, tb=256):
    T = proj.shape[0]
    qb = (3 * CONV_W) // DN_W
    n_chunks = T // CHUNK
    return pl.pallas_call(
        _gdn_kernel,
        grid=(T // tb,),
        in_specs=[pl.BlockSpec((tb, DN_W), lambda t: (t, qb)),
                  pl.BlockSpec((tb, DN_W), lambda t: (t, qb + 1)),
                  pl.BlockSpec((tb, DN_W), lambda t: (t, qb + 2)),
                  pl.BlockSpec((tb, DN_W), lambda t: (t, qb + 3)),
                  pl.BlockSpec((tb, N_HEADS), lambda t: (t, 0)),
                  pl.BlockSpec((tb, N_HEADS), lambda t: (t, 0)),
                  pl.BlockSpec((N_HEADS, n_chunks, CHUNK), lambda t: (0, 0, 0)),
                  pl.BlockSpec((4, 3 * DN_W), lambda t: (0, 0)),
                  pl.BlockSpec((1, HEAD_DIM), lambda t: (0, 0))],
        out_specs=pl.BlockSpec((tb, DN_W), lambda t: (t, 0)),
        out_shape=jax.ShapeDtypeStruct((T, DN_W), BF16),
        scratch_shapes=[pltpu.VMEM((tb, DN_W), F32),
                        pltpu.VMEM((tb, DN_W), F32),
                        pltpu.VMEM((tb, DN_W), F32),
                        pltpu.VMEM((3, 8, DN_W), F32),
                        pltpu.VMEM((N_HEADS, HEAD_DIM, HEAD_DIM), F32)],
        compiler_params=_cparams(("arbitrary",)),
        name="gdn",
    )(proj, proj, proj, proj, gc, beta, gct, conv_w, norm_w)


def _out_proj_kernel(ya_ref, yb_ref, w_ref, x_ref, o_ref, wb_ref):
    @pl.when(pl.program_id(1) == 0)
    def _():
        wb_ref[...] = w_ref[...].astype(BF16)

    o_ref[...] = (x_ref[...]
                  + jnp.dot(ya_ref[...], wb_ref[:CONV_W], preferred_element_type=F32)
                  + jnp.dot(yb_ref[...], wb_ref[CONV_W:], preferred_element_type=F32))


def _out_proj(ya, yb, w_out, x, bm=1024, bn=512):
    T, D = x.shape
    K = w_out.shape[0]
    return pl.pallas_call(
        _out_proj_kernel,
        grid=(D // bn, T // bm),
        in_specs=[pl.BlockSpec((bm, CONV_W), lambda j, i: (i, 0)),
                  pl.BlockSpec((bm, DN_W), lambda j, i: (i, 0)),
                  pl.BlockSpec((K, bn), lambda j, i: (0, j)),
                  pl.BlockSpec((bm, bn), lambda j, i: (i, j))],
        out_specs=pl.BlockSpec((bm, bn), lambda j, i: (i, j)),
        out_shape=jax.ShapeDtypeStruct((T, D), F32),
        scratch_shapes=[pltpu.VMEM((K, bn), BF16)],
        compiler_params=_cparams(("arbitrary", "arbitrary")),
        name="out_proj",
    )(ya, yb, w_out, x)


def _router_kernel(h_ref, nw_ref, wr_ref, br_ref, xp_ref, idx_ref, gate_ref, rank_ref, cnt_ref,
                   carry_ref):
    bm, D = h_ref.shape

    @pl.when(pl.program_id(0) == 0)
    def _():
        carry_ref[...] = jnp.zeros(carry_ref.shape, F32)

    h = h_ref[...]
    ms = jnp.mean(h * h, axis=-1, keepdims=True)
    xn = h * lax.rsqrt(ms + EPS) * nw_ref[...]
    xp_ref[...] = pltpu.pack_elementwise([xn[:, :D // 2], xn[:, D // 2:]], packed_dtype=BF16)

    logits = _mm3(xn, wr_ref[...]) + br_ref[...]
    lane_e = lax.broadcasted_iota(I32, logits.shape, 1)
    work = logits
    sels, vals, idxs = [], [], []
    for _ in range(TOP_K):
        m = jnp.max(work, axis=-1, keepdims=True)
        idx = jnp.min(jnp.where(work == m, lane_e, N_EXPERTS), axis=-1, keepdims=True)
        sel = lane_e == idx
        work = jnp.where(sel, -jnp.inf, work)
        sels.append(sel)
        vals.append(m)
        idxs.append(idx)
    exps = [jnp.exp(v - vals[0]) for v in vals]
    denom = exps[0] + exps[1] + exps[2] + exps[3]
    gates = [e / denom for e in exps]

    onehot = jnp.zeros(logits.shape, F32)
    for sel in sels:
        onehot = onehot + sel.astype(F32)
    ri = lax.broadcasted_iota(I32, (bm, bm), 0)
    ci = lax.broadcasted_iota(I32, (bm, bm), 1)
    tri = (ri > ci).astype(BF16)
    carry = carry_ref[...]
    before = jnp.dot(tri, onehot.astype(BF16), preferred_element_type=F32) + carry
    ranks = [jnp.sum(jnp.where(sel, before, 0.0), axis=-1, keepdims=True).astype(I32)
             for sel in sels]
    new_carry = carry + jnp.sum(onehot, axis=0, keepdims=True)
    carry_ref[...] = new_carry
    cnt_ref[...] = new_carry

    lane = lax.broadcasted_iota(I32, (bm, LANE), 1)

    def spread(cols):
        return jnp.where(lane == 0, cols[0],
                         jnp.where(lane == 1, cols[1], jnp.where(lane == 2, cols[2], cols[3])))

    idx_ref[...] = spread(idxs)
    gate_ref[...] = spread(gates)
    rank_ref[...] = spread(ranks)


def _router(h1, nw, w_router, b_router, bm=256):
    T, D = h1.shape
    return pl.pallas_call(
        _router_kernel,
        grid=(T // bm,),
        in_specs=[pl.BlockSpec((bm, D), lambda i: (i, 0)),
                  pl.BlockSpec((1, D), lambda i: (0, 0)),
                  pl.BlockSpec((D, N_EXPERTS), lambda i: (0, 0)),
                  pl.BlockSpec((1, N_EXPERTS), lambda i: (0, 0))],
        out_specs=[pl.BlockSpec((bm, D // 2), lambda i: (i, 0)),
                   pl.BlockSpec((bm, LANE), lambda i: (i, 0)),
                   pl.BlockSpec((bm, LANE), lambda i: (i, 0)),
                   pl.BlockSpec((bm, LANE), lambda i: (i, 0)),
                   pl.BlockSpec((1, N_EXPERTS), lambda i: (0, 0))],
        out_shape=[jax.ShapeDtypeStruct((T, D // 2), jnp.uint32),
                   jax.ShapeDtypeStruct((T, LANE), I32),
                   jax.ShapeDtypeStruct((T, LANE), F32),
                   jax.ShapeDtypeStruct((T, LANE), I32),
                   jax.ShapeDtypeStruct((1, N_EXPERTS), F32)],
        scratch_shapes=[pltpu.VMEM((1, N_EXPERTS), F32)],
        compiler_params=_cparams(("arbitrary",)),
        name="router",
    )(h1, nw, w_router, b_router)


def _dispatch_kernel(dest_ref, pad_start_ref, pad_len_ref, nused_ref, x_ref, xs_ref, zero_ref, sem):
    tb = x_ref.shape[0]
    i = pl.program_id(0)
    n_blocks = xs_ref.shape[0] // SLOT_BLK

    def row_copy(r, slot):
        return pltpu.make_async_copy(x_ref.at[pl.ds(r, 1)], xs_ref.at[pl.ds(slot, 1)], sem)

    def zero_row_copy(slot):
        return pltpu.make_async_copy(zero_ref.at[pl.ds(0, 1)], xs_ref.at[pl.ds(slot, 1)], sem)

    def zero_blk_copy(b):
        start = pl.multiple_of(b * SLOT_BLK, SLOT_BLK)
        return pltpu.make_async_copy(zero_ref, xs_ref.at[pl.ds(start, SLOT_BLK)], sem)

    @pl.when(i == 0)
    def _():
        zero_ref[...] = jnp.zeros(zero_ref.shape, zero_ref.dtype)

        def per_expert(e, c):
            start = pad_start_ref[e]
            n = pad_len_ref[e]
            lax.fori_loop(0, n, lambda j, cc: (zero_row_copy(start + j).start(), cc)[1], 0)
            lax.fori_loop(0, n, lambda j, cc: (zero_row_copy(start + j).wait(), cc)[1], 0)
            return c

        lax.fori_loop(0, N_EXPERTS, per_expert, 0)
        nu = nused_ref[0]
        lax.fori_loop(nu, n_blocks, lambda b, cc: (zero_blk_copy(b).start(), cc)[1], 0)
        lax.fori_loop(nu, n_blocks, lambda b, cc: (zero_blk_copy(b).wait(), cc)[1], 0)

    def issue(r, c):
        base = (i * tb + r) * TOP_K
        for k in range(TOP_K):
            row_copy(r, dest_ref[base + k]).start()
        return c

    def drain(r, c):
        for k in range(TOP_K):
            row_copy(0, 0).wait()
        return c

    lax.fori_loop(0, tb, issue, 0)
    lax.fori_loop(0, tb, drain, 0)


def _dispatch(dest_flat, pad_start, pad_len, nused, xp, n_slots, tb=256):
    T, W = xp.shape
    return pl.pallas_call(
        _dispatch_kernel,
        grid_spec=pltpu.PrefetchScalarGridSpec(
            num_scalar_prefetch=4,
            grid=(T // tb,),
            in_specs=[pl.BlockSpec((tb, W), lambda i, *_: (i, 0))],
            out_specs=pl.BlockSpec(memory_space=pl.ANY),
            scratch_shapes=[pltpu.VMEM((SLOT_BLK, W), xp.dtype),
                            pltpu.SemaphoreType.DMA(())],
        ),
        out_shape=jax.ShapeDtypeStruct((n_slots, W), xp.dtype),
        compiler_params=_cparams(("arbitrary",)),
        name="dispatch",
    )(dest_flat, pad_start, pad_len, nused, xp)


def _unpack_pair(xw):
    xa = pltpu.unpack_elementwise(xw, index=0, packed_dtype=BF16, unpacked_dtype=F32)
    xb = pltpu.unpack_elementwise(xw, index=1, packed_dtype=BF16, unpacked_dtype=F32)
    return xa.astype(BF16), xb.astype(BF16)


def _moe_up_kernel(be_ref, nu_ref, x_ref, wg_ref, wu_ref, bg_ref, bu_ref, h_ref):
    b = pl.program_id(1)
    half = wg_ref.shape[0] // 2

    @pl.when(b < nu_ref[0])
    def _():
        xa, xb = _unpack_pair(x_ref[...])

        def proj(w_ref, b_ref):
            return (jnp.dot(xa, w_ref[:half].astype(BF16), preferred_element_type=F32)
                    + jnp.dot(xb, w_ref[half:].astype(BF16), preferred_element_type=F32)
                    + b_ref[...])

        gate = jnp.minimum(proj(wg_ref, bg_ref), SWIGLU_LIMIT)
        up = jnp.clip(proj(wu_ref, bu_ref), -SWIGLU_LIMIT, SWIGLU_LIMIT)
        glu = gate * _sigmoid(SWIGLU_ALPHA * gate)
        h_ref[...] = ((up + 1.0) * glu).astype(BF16)

    @pl.when(b >= nu_ref[0])
    def _():
        h_ref[...] = jnp.zeros(h_ref.shape, h_ref.dtype)


def _moe_up(block_expert, nused, xs, w_gate_up, b_gate_up, tf=512):
    n_slots, W = xs.shape
    D = 2 * W
    nb = n_slots // SLOT_BLK
    nf = EXPERT_FF // tf

    def clamp(b, nu):
        return jnp.minimum(b, nu[0] - 1)

    return pl.pallas_call(
        _moe_up_kernel,
        grid_spec=pltpu.PrefetchScalarGridSpec(
            num_scalar_prefetch=2,
            grid=(nf, nb),
            in_specs=[
                pl.BlockSpec((SLOT_BLK, W), lambda f, b, be, nu: (clamp(b, nu), 0)),
                pl.BlockSpec((None, None, D, tf), lambda f, b, be, nu: (0, be[clamp(b, nu)], 0, f)),
                pl.BlockSpec((None, None, D, tf),
                             lambda f, b, be, nu: (0, be[clamp(b, nu)], 0, f + nf)),
                pl.BlockSpec((None, 1, tf), lambda f, b, be, nu: (be[clamp(b, nu)], 0, f)),
                pl.BlockSpec((None, 1, tf), lambda f, b, be, nu: (be[clamp(b, nu)], 0, f + nf)),
            ],
            out_specs=pl.BlockSpec((SLOT_BLK, tf), lambda f, b, be, nu: (b, f)),
        ),
        out_shape=jax.ShapeDtypeStruct((n_slots, EXPERT_FF), BF16),
        compiler_params=_cparams(("arbitrary", "arbitrary")),
        name="moe_up",
    )(block_expert, nused, xs, w_gate_up, w_gate_up, b_gate_up, b_gate_up)


def _moe_down_kernel(be_ref, nu_ref, h_ref, w_ref, b_ref, y_ref):
    b = pl.program_id(1)

    @pl.when(b < nu_ref[0])
    def _():
        y_ref[...] = (jnp.dot(h_ref[...], w_ref[...].astype(BF16), preferred_element_type=F32)
                      + b_ref[...])

    @pl.when(b >= nu_ref[0])
    def _():
        y_ref[...] = jnp.zeros(y_ref.shape, y_ref.dtype)


def _moe_down(block_expert, nused, hid, w_down, b_down, tn=1024):
    n_slots, FF = hid.shape
    D = w_down.shape[-1]
    nb = n_slots // SLOT_BLK

    def clamp(b, nu):
        return jnp.minimum(b, nu[0] - 1)

    return pl.pallas_call(
        _moe_down_kernel,
        grid_spec=pltpu.PrefetchScalarGridSpec(
            num_scalar_prefetch=2,
            grid=(D // tn, nb),
            in_specs=[
                pl.BlockSpec((SLOT_BLK, FF), lambda n, b, be, nu: (clamp(b, nu), 0)),
                pl.BlockSpec((None, None, FF, tn), lambda n, b, be, nu: (0, be[clamp(b, nu)], 0, n)),
                pl.BlockSpec((None, 1, tn), lambda n, b, be, nu: (be[clamp(b, nu)], 0, n)),
            ],
            out_specs=pl.BlockSpec((SLOT_BLK, tn), lambda n, b, be, nu: (b, n)),
        ),
        out_shape=jax.ShapeDtypeStruct((n_slots, D), F32),
        compiler_params=_cparams(("arbitrary", "arbitrary")),
        name="moe_down",
    )(block_expert, nused, hid, w_down, b_down)


def _combine_kernel(dest_ref, h_ref, gate_ref, ys_ref, nw_ref, o_ref, buf_ref, sem):
    tb = h_ref.shape[0]
    i = pl.program_id(0)

    def row_copy(slot, k, r):
        return pltpu.make_async_copy(ys_ref.at[pl.ds(slot, 1)], buf_ref.at[k, pl.ds(r, 1)], sem)

    def issue(r, c):
        base = (i * tb + r) * TOP_K
        for k in range(TOP_K):
            row_copy(dest_ref[base + k], k, r).start()
        return c

    def drain(r, c):
        for k in range(TOP_K):
            row_copy(0, k, r).wait()
        return c

    lax.fori_loop(0, tb, issue, 0)
    lax.fori_loop(0, tb, drain, 0)

    gates = gate_ref[...]
    h = h_ref[...]
    for k in range(TOP_K):
        h = h + buf_ref[k] * gates[:, k:k + 1]
    ms = jnp.mean(h * h, axis=-1, keepdims=True)
    o_ref[...] = h * lax.rsqrt(ms + EPS) * nw_ref[...]


def _combine(dest_flat, h1, gates, ys, nw, tb=128):
    T, D = h1.shape
    return pl.pallas_call(
        _combine_kernel,
        grid_spec=pltpu.PrefetchScalarGridSpec(
            num_scalar_prefetch=1,
            grid=(T // tb,),
            in_specs=[pl.BlockSpec((tb, D), lambda i, *_: (i, 0)),
                      pl.BlockSpec((tb, LANE), lambda i, *_: (i, 0)),
                      pl.BlockSpec(memory_space=pl.ANY),
                      pl.BlockSpec((1, D), lambda i, *_: (0, 0))],
            out_specs=pl.BlockSpec((tb, D), lambda i, *_: (i, 0)),
            scratch_shapes=[pltpu.VMEM((TOP_K, tb, D), F32),
                            pltpu.SemaphoreType.DMA(())],
        ),
        out_shape=jax.ShapeDtypeStruct((T, D), F32),
        compiler_params=_cparams(("arbitrary",)),
        name="combine",
    )(dest_flat, h1, gates, ys, nw)


def _layer(h, norm1_w, w_in, conv_a_w, conv_a_norm_w, dn_conv_w, dn_A_log, dn_dt_bias, dn_norm_w,
           w_out, norm2_w, w_router, b_router, w_gate_up, b_gate_up, w_down, b_down, l):
    T, D = h.shape
    main_cols = 3 * CONV_W + 4 * DN_W
    wab = w_in[l][:, main_cols:]
    xn, gc, beta = _norm1(h, norm1_w[l][None], wab, dn_A_log[l][None], dn_dt_bias[l][None])
    proj = _in_proj(xn, w_in[l:l + 1], main_cols)
    ya = _conv_mix(proj, conv_a_w[l], conv_a_norm_w[l][None])
    gct = gc.T.reshape(N_HEADS, T // CHUNK, CHUNK)
    yb = _gdn(proj, gc, beta, gct, dn_conv_w[l], dn_norm_w[l][None])
    h1 = _out_proj(ya, yb, w_out[l], h)

    xp, idx, gates, rank, counts = _router(h1, norm2_w[l][None], w_router[l], b_router[l][None])
    top_idx = idx[:, :TOP_K]
    counts = counts[0].astype(I32)
    padded = ((counts + SLOT_BLK - 1) // SLOT_BLK) * SLOT_BLK
    pend = jnp.cumsum(padded)
    pstart = pend - padded
    dest_flat = (pstart[top_idx] + rank[:, :TOP_K]).reshape(-1)
    n_slots = T * TOP_K + N_EXPERTS * SLOT_BLK
    nb = n_slots // SLOT_BLK
    nused = (pend[-1] // SLOT_BLK).reshape(1)
    block_expert = jnp.clip(
        jnp.searchsorted(pend, jnp.arange(nb, dtype=I32) * SLOT_BLK, side="right"),
        0, N_EXPERTS - 1).astype(I32)

    xs = _dispatch(dest_flat, pstart + counts, padded - counts, nused, xp, n_slots)
    hid = _moe_up(block_expert, nused, xs, w_gate_up[l:l + 1], b_gate_up[l][:, None, :])
    ys = _moe_down(block_expert, nused, hid, w_down[l:l + 1], b_down[l][:, None, :])
    return h1, dest_flat, gates, ys


def kernel(x, norm1_w, w_in, conv_a_w, conv_a_norm_w, dn_conv_w, dn_A_log, dn_dt_bias, dn_norm_w,
           w_out, norm2_w, w_router, b_router, w_gate_up, b_gate_up, w_down, b_down, final_norm_w):
    B, S, D = x.shape
    depth = norm1_w.shape[0]
    assert depth == 1, "the final norm is fused into the last layer's combine"
    h = x.reshape(B * S, D)
    h1, dest_flat, gates, ys = _layer(
        h, norm1_w, w_in, conv_a_w, conv_a_norm_w, dn_conv_w, dn_A_log, dn_dt_bias, dn_norm_w,
        w_out, norm2_w, w_router, b_router, w_gate_up, b_gate_up, w_down, b_down, 0)
    out = _combine(dest_flat, h1, gates, ys, final_norm_w[None])
    return out.reshape(B, S, D)
```

```python
import functools

import jax
import jax.numpy as jnp
from jax import lax
from jax.experimental import pallas as pl
from jax.experimental.pallas import tpu as pltpu

F32 = jnp.float32
BF16 = jnp.bfloat16
I32 = jnp.int32

EPS = 1e-6
CHUNK = 64
HEAD_DIM = 128
N_HEADS = 16
CONV_W = 2048
DN_W = 2048
N_EXPERTS = 32
TOP_K = 4
EXPERT_FF = 2048
SWIGLU_LIMIT = 7.0
SWIGLU_ALPHA = 1.702

LANE = 128
VMEM_LIMIT = 56 * 1024 * 1024
SLOT_BLK = 256
MOE_SUB = SLOT_BLK
MOE_ITEM_SUBS = 5


def _cparams(sem):
    return pltpu.CompilerParams(dimension_semantics=sem, vmem_limit_bytes=VMEM_LIMIT)


def _mm(a, b):
    return jnp.dot(a.astype(BF16), b.astype(BF16), preferred_element_type=F32)


def _dot_nt(a, b):
    return lax.dot_general(a, b, (((1,), (1,)), ((), ())), preferred_element_type=F32)


def _mm_nt(a, b):
    return _dot_nt(a.astype(BF16), b.astype(BF16))


def _mm_tn(a, b):
    return lax.dot_general(a.astype(BF16), b.astype(BF16), (((0,), (0,)), ((), ())),
                           preferred_element_type=F32)


def _mm3_nt(x, wt):
    xh = x.astype(BF16)
    xl = (x - xh.astype(F32)).astype(BF16)
    wh = wt.astype(BF16)
    wl = (wt - wh.astype(F32)).astype(BF16)
    return _dot_nt(xh, wh) + _dot_nt(xl, wh) + _dot_nt(xh, wl)


def _sigmoid(x):
    return 1.0 / (1.0 + jnp.exp(-x))


def _norm1_kernel(x_ref, nw_ref, wab_ref, alog_ref, dtb_ref, xn_ref, gc_ref, beta_ref):
    x = x_ref[...]
    ms = jnp.mean(x * x, axis=-1, keepdims=True)
    xn = x * lax.rsqrt(ms + EPS) * nw_ref[...]
    xn_ref[...] = xn.astype(BF16)
    ab = _mm3_nt(xn, wab_ref[...])
    a = ab[:, :N_HEADS] + dtb_ref[...]
    b = ab[:, N_HEADS:]
    softplus = jnp.maximum(a, 0.0) + jnp.log1p(jnp.exp(-jnp.abs(a)))
    g = -jnp.exp(alog_ref[...]) * softplus
    pos = lax.broadcasted_iota(I32, g.shape, 0) % CHUNK
    shift = 1
    while shift < CHUNK:
        g = g + jnp.where(pos >= shift, pltpu.roll(g, shift, 0), 0.0)
        shift *= 2
    gc_ref[...] = g
    beta_ref[...] = _sigmoid(b)


def _norm1(x, nw, w_in_t, ab_row, alog, dtb, bm=256):
    T, D = x.shape
    nab = 2 * N_HEADS
    return pl.pallas_call(
        _norm1_kernel,
        grid=(T // bm,),
        in_specs=[pl.BlockSpec((bm, D), lambda i: (i, 0)),
                  pl.BlockSpec((1, D), lambda i: (0, 0)),
                  pl.BlockSpec((nab, D), lambda i: (ab_row // nab, 0)),
                  pl.BlockSpec((1, N_HEADS), lambda i: (0, 0)),
                  pl.BlockSpec((1, N_HEADS), lambda i: (0, 0))],
        out_specs=[pl.BlockSpec((bm, D), lambda i: (i, 0)),
                   pl.BlockSpec((bm, N_HEADS), lambda i: (i, 0)),
                   pl.BlockSpec((bm, N_HEADS), lambda i: (i, 0))],
        out_shape=[jax.ShapeDtypeStruct((T, D), BF16),
                   jax.ShapeDtypeStruct((T, N_HEADS), F32),
                   jax.ShapeDtypeStruct((T, N_HEADS), F32)],
        compiler_params=_cparams(("arbitrary",)),
        name="norm1",
    )(x, nw, w_in_t, alog, dtb)


def _in_proj_kernel(x_ref, w_ref, o_ref, wb_ref):
    @pl.when(pl.program_id(1) == 0)
    def _():
        wb_ref[...] = w_ref[...].astype(BF16)

    o_ref[...] = _dot_nt(x_ref[...], wb_ref[...])


def _in_proj(xn, w_in_t, n_cols, bm=1024, bn=512):
    T, D = xn.shape
    return pl.pallas_call(
        _in_proj_kernel,
        grid=(n_cols // bn, T // bm),
        in_specs=[pl.BlockSpec((bm, D), lambda j, i: (i, 0)),
                  pl.BlockSpec((bn, D), lambda j, i: (j, 0))],
        out_specs=pl.BlockSpec((bm, bn), lambda j, i: (i, j)),
        out_shape=jax.ShapeDtypeStruct((T, n_cols), F32),
        scratch_shapes=[pltpu.VMEM((bn, D), BF16)],
        compiler_params=_cparams(("arbitrary", "arbitrary")),
        name="in_proj",
    )(xn, w_in_t)


def _conv_mix_kernel(x_ref, b_ref, c_ref, xh_ref, ch_ref, cw_ref, nw_ref, o_ref):
    bm, bc = x_ref.shape
    u = c_ref[...] * x_ref[...]
    halo = jnp.where(pl.program_id(0) > 0, ch_ref[...] * xh_ref[...], 0.0)
    ext = jnp.concatenate([halo, u], axis=0)
    cw = cw_ref[...]
    conv = ext[6:6 + bm] * cw[0:1] + ext[7:7 + bm] * cw[1:2] + u * cw[2:3]
    y = b_ref[...] * conv
    nw = nw_ref[...]
    for g in range(bc // LANE):
        cs = slice(g * LANE, (g + 1) * LANE)
        yg = y[:, cs]
        ms = jnp.mean(yg * yg, axis=-1, keepdims=True)
        o_ref[:, cs] = (yg * lax.rsqrt(ms + EPS) * nw[:, cs]).astype(BF16)


def _conv_mix(proj, conv_w, norm_w, bm=512, bc=512):
    T = proj.shape[0]
    nc = CONV_W // bc
    hb = bm // 8
    halo_map = lambda off: (lambda i, j: (jnp.maximum(i * hb - 1, 0), j + off))
    return pl.pallas_call(
        _conv_mix_kernel,
        grid=(T // bm, nc),
        in_specs=[pl.BlockSpec((bm, bc), lambda i, j: (i, j)),
                  pl.BlockSpec((bm, bc), lambda i, j: (i, j + nc)),
                  pl.BlockSpec((bm, bc), lambda i, j: (i, j + 2 * nc)),
                  pl.BlockSpec((8, bc), halo_map(0)),
                  pl.BlockSpec((8, bc), halo_map(2 * nc)),
                  pl.BlockSpec((3, bc), lambda i, j: (0, j)),
                  pl.BlockSpec((1, bc), lambda i, j: (0, j))],
        out_specs=pl.BlockSpec((bm, bc), lambda i, j: (i, j)),
        out_shape=jax.ShapeDtypeStruct((T, CONV_W), BF16),
        compiler_params=_cparams(("arbitrary", "arbitrary")),
        name="conv_mix",
    )(proj, proj, proj, proj, proj, conv_w, norm_w)


def _gdn_kernel(q_ref, k_ref, v_ref, z_ref, gc_ref, beta_ref, gct_ref, cw_ref, nw_ref,
                y_ref, qs_ref, ks_ref, vs_ref, halo_ref, s_ref):
    tb = q_ref.shape[0]
    t = pl.program_id(0)

    @pl.when(t == 0)
    def _():
        halo_ref[...] = jnp.zeros(halo_ref.shape, F32)
        s_ref[...] = jnp.zeros(s_ref.shape, F32)

    for a, (src, dst) in enumerate(((q_ref, qs_ref), (k_ref, ks_ref), (v_ref, vs_ref))):
        for h in range(N_HEADS):
            cs = slice(h * HEAD_DIM, (h + 1) * HEAD_DIM)
            raw = src[:, cs]
            ext = jnp.concatenate([halo_ref[a, :, cs], raw], axis=0)
            cw = cw_ref[:, a * DN_W + h * HEAD_DIM:a * DN_W + (h + 1) * HEAD_DIM]
            c = (ext[5:5 + tb] * cw[0:1] + ext[6:6 + tb] * cw[1:2]
                 + ext[7:7 + tb] * cw[2:3] + raw * cw[3:4])
            c = c * _sigmoid(c)
            if a < 2:
                c = c * lax.rsqrt(jnp.sum(c * c, axis=-1, keepdims=True) + EPS)
            if a == 0:
                c = c * (HEAD_DIM ** -0.5)
            dst[:, cs] = c
            halo_ref[a, :, cs] = raw[tb - 8:tb]

    ii = lax.broadcasted_iota(I32, (CHUNK, CHUNK), 0)
    jj = lax.broadcasted_iota(I32, (CHUNK, CHUNK), 1)
    eye = (ii == jj).astype(F32)
    nw = nw_ref[...]

    heads = range(N_HEADS)
    cols = [slice(h * HEAD_DIM, (h + 1) * HEAD_DIM) for h in heads]

    def chunk(c, carry):
        rows = pl.ds(pl.multiple_of(c * CHUNK, CHUNK), CHUNK)
        cg = t * (tb // CHUNK) + c
        gcb = gc_ref[rows, :]
        btb = beta_ref[rows, :]
        q = [qs_ref[rows, cs] for cs in cols]
        k = [ks_ref[rows, cs] for cs in cols]
        v = [vs_ref[rows, cs] for cs in cols]
        z = [z_ref[rows, cs] for cs in cols]
        s = [s_ref[h] for h in heads]
        gcol = [gcb[:, h:h + 1] for h in heads]
        bcol = [btb[:, h:h + 1] for h in heads]
        grow = [gct_ref[h, pl.ds(cg, 1), :] for h in heads]
        gl = [g[:, CHUNK - 1:CHUNK] for g in grow]
        eg = [jnp.exp(g) for g in gcol]
        kb = [k[h] * bcol[h] for h in heads]
        a1 = [_mm_nt(jnp.concatenate([kb[h], q[h]], axis=0), k[h]) for h in heads]
        dec = [jnp.where(ii >= jj, jnp.exp(jnp.minimum(gcol[h] - grow[h], 0.0)), 0.0) for h in heads]
        lmat = [jnp.where(ii > jj, a1[h][:CHUNK] * dec[h], 0.0) for h in heads]
        qkd = [a1[h][CHUNK:] * dec[h] for h in heads]
        p = [eye - lmat[h] for h in heads]
        cur = [_mm(lmat[h], lmat[h]) for h in heads]
        power = 2
        while 2 * power < CHUNK:
            r = [_mm(jnp.concatenate([p[h], cur[h]], axis=0), cur[h]) for h in heads]
            p = [p[h] + r[h][:CHUNK] for h in heads]
            cur = [r[h][CHUNK:] for h in heads]
            power *= 2
        p = [p[h] + _mm(p[h], cur[h]) for h in heads]
        uw = [_mm(p[h], jnp.concatenate([v[h] * bcol[h], kb[h] * eg[h]], axis=1)) for h in heads]
        ws = [_mm(jnp.concatenate([uw[h][:, HEAD_DIM:], q[h] * eg[h]], axis=0), s[h]) for h in heads]
        vn = [uw[h][:, :HEAD_DIM] - ws[h][:CHUNK] for h in heads]
        o = [ws[h][CHUNK:] + _mm(qkd[h], vn[h]) for h in heads]
        kt = [k[h] * jnp.exp(gl[h] - gcol[h]) for h in heads]
        s_new = [s[h] * jnp.exp(gl[h]) + _mm_tn(kt[h], vn[h]) for h in heads]
        for h in heads:
            on = o[h] * lax.rsqrt(jnp.mean(o[h] * o[h], axis=-1, keepdims=True) + EPS) * nw
            y_ref[rows, cols[h]] = (on * (z[h] * _sigmoid(z[h]))).astype(BF16)
            s_ref[h] = s_new[h]
        return carry

    lax.fori_loop(0, tb // CHUNK, chunk, 0)


def _gdn(proj, gc, beta, gct, conv_w, norm_w, tb=256):
    T = proj.shape[0]
    qb = (3 * CONV_W) // DN_W
    n_chunks = T // CHUNK
    return pl.pallas_call(
        _gdn_kernel,
        grid=(T // tb,),
        in_specs=[pl.BlockSpec((tb, DN_W), lambda t: (t, qb)),
                  pl.BlockSpec((tb, DN_W), lambda t: (t, qb + 1)),
                  pl.BlockSpec((tb, DN_W), lambda t: (t, qb + 2)),
                  pl.BlockSpec((tb, DN_W), lambda t: (t, qb + 3)),
                  pl.BlockSpec((tb, N_HEADS), lambda t: (t, 0)),
                  pl.BlockSpec((tb, N_HEADS), lambda t: (t, 0)),
                  pl.BlockSpec((N_HEADS, n_chunks, CHUNK), lambda t: (0, 0, 0)),
                  pl.BlockSpec((4, 3 * DN_W), lambda t: (0, 0)),
                  pl.BlockSpec((1, HEAD_DIM), lambda t: (0, 0))],
        out_specs=pl.BlockSpec((tb, DN_W), lambda t: (t, 0)),
        out_shape=jax.ShapeDtypeStruct((T, DN_W), BF16),
        scratch_shapes=[pltpu.VMEM((tb, DN_W), F32),
                        pltpu.VMEM((tb, DN_W), F32),
                        pltpu.VMEM((tb, DN_W), F32),
                        pltpu.VMEM((3, 8, DN_W), F32),
                        pltpu.VMEM((N_HEADS, HEAD_DIM, HEAD_DIM), F32)],
        compiler_params=_cparams(("arbitrary",)),
        name="gdn",
    )(proj, proj, proj, proj, gc, beta, gct, conv_w, norm_w)


def _out_proj_kernel(ya_ref, yb_ref, w_ref, x_ref, o_ref, wb_ref):
    @pl.when(pl.program_id(1) == 0)
    def _():
        wb_ref[...] = w_ref[...].astype(BF16)

    o_ref[...] = (x_ref[...]
                  + jnp.dot(ya_ref[...], wb_ref[:CONV_W], preferred_element_type=F32)
                  + jnp.dot(yb_ref[...], wb_ref[CONV_W:], preferred_element_type=F32))


def _out_proj(ya, yb, w_out, x, bm=1024, bn=512):
    T, D = x.shape
    K = w_out.shape[0]
    return pl.pallas_call(
        _out_proj_kernel,
        grid=(D // bn, T // bm),
        in_specs=[pl.BlockSpec((bm, CONV_W), lambda j, i: (i, 0)),
                  pl.BlockSpec((bm, DN_W), lambda j, i: (i, 0)),
                  pl.BlockSpec((K, bn), lambda j, i: (0, j)),
                  pl.BlockSpec((bm, bn), lambda j, i: (i, j))],
        out_specs=pl.BlockSpec((bm, bn), lambda j, i: (i, j)),
        out_shape=jax.ShapeDtypeStruct((T, D), F32),
        scratch_shapes=[pltpu.VMEM((K, bn), BF16)],
        compiler_params=_cparams(("arbitrary", "arbitrary")),
        name="out_proj",
    )(ya, yb, w_out, x)


def _router_kernel(h_ref, nw_ref, wr_ref, br_ref, xp_ref, idx_ref, gate_ref, rank_ref, cnt_ref,
                   carry_ref):
    bm, D = h_ref.shape

    @pl.when(pl.program_id(0) == 0)
    def _():
        carry_ref[...] = jnp.zeros(carry_ref.shape, F32)

    h = h_ref[...]
    ms = jnp.mean(h * h, axis=-1, keepdims=True)
    xn = h * lax.rsqrt(ms + EPS) * nw_ref[...]
    xp_ref[...] = pltpu.pack_elementwise([xn[:, :D // 2], xn[:, D // 2:]], packed_dtype=BF16)

    logits = _mm3_nt(xn, wr_ref[...]) + br_ref[...]
    lane_e = lax.broadcasted_iota(I32, logits.shape, 1)
    work = logits
    sels, vals, idxs = [], [], []
    for _ in range(TOP_K):
        m = jnp.max(work, axis=-1, keepdims=True)
        idx = jnp.min(jnp.where(work == m, lane_e, N_EXPERTS), axis=-1, keepdims=True)
        sel = lane_e == idx
        work = jnp.where(sel, -jnp.inf, work)
        sels.append(sel)
        vals.append(m)
        idxs.append(idx)
    exps = [jnp.exp(v - vals[0]) for v in vals]
    denom = exps[0] + exps[1] + exps[2] + exps[3]
    gates = [e / denom for e in exps]

    onehot = jnp.zeros(logits.shape, F32)
    for sel in sels:
        onehot = onehot + sel.astype(F32)
    ri = lax.broadcasted_iota(I32, (bm, bm), 0)
    ci = lax.broadcasted_iota(I32, (bm, bm), 1)
    tri = (ri > ci).astype(BF16)
    carry = carry_ref[...]
    before = jnp.dot(tri, onehot.astype(BF16), preferred_element_type=F32) + carry
    ranks = [jnp.sum(jnp.where(sel, before, 0.0), axis=-1, keepdims=True).astype(I32)
             for sel in sels]
    new_carry = carry + jnp.sum(onehot, axis=0, keepdims=True)
    carry_ref[...] = new_carry
    cnt_ref[...] = new_carry

    lane = lax.broadcasted_iota(I32, (bm, LANE), 1)

    def spread(cols):
        return jnp.where(lane == 0, cols[0],
                         jnp.where(lane == 1, cols[1], jnp.where(lane == 2, cols[2], cols[3])))

    idx_ref[...] = spread(idxs)
    gate_ref[...] = spread(gates)
    rank_ref[...] = spread(ranks)


def _router(h1, nw, w_router_t, b_router, bm=256):
    T, D = h1.shape
    return pl.pallas_call(
        _router_kernel,
        grid=(T // bm,),
        in_specs=[pl.BlockSpec((bm, D), lambda i: (i, 0)),
                  pl.BlockSpec((1, D), lambda i: (0, 0)),
                  pl.BlockSpec((N_EXPERTS, D), lambda i: (0, 0)),
                  pl.BlockSpec((1, N_EXPERTS), lambda i: (0, 0))],
        out_specs=[pl.BlockSpec((bm, D // 2), lambda i: (i, 0)),
                   pl.BlockSpec((bm, LANE), lambda i: (i, 0)),
                   pl.BlockSpec((bm, LANE), lambda i: (i, 0)),
                   pl.BlockSpec((bm, LANE), lambda i: (i, 0)),
                   pl.BlockSpec((1, N_EXPERTS), lambda i: (0, 0))],
        out_shape=[jax.ShapeDtypeStruct((T, D // 2), jnp.uint32),
                   jax.ShapeDtypeStruct((T, LANE), I32),
                   jax.ShapeDtypeStruct((T, LANE), F32),
                   jax.ShapeDtypeStruct((T, LANE), I32),
                   jax.ShapeDtypeStruct((1, N_EXPERTS), F32)],
        scratch_shapes=[pltpu.VMEM((1, N_EXPERTS), F32)],
        compiler_params=_cparams(("arbitrary",)),
        name="router",
    )(h1, nw, w_router_t, b_router)


def _dispatch_kernel(dest_ref, pad_start_ref, pad_len_ref, nused_ref, x_ref, xs_ref, zero_ref, sem):
    tb = x_ref.shape[0]
    i = pl.program_id(0)
    n_blocks = xs_ref.shape[0] // SLOT_BLK

    def row_copy(r, slot):
        return pltpu.make_async_copy(x_ref.at[pl.ds(r, 1)], xs_ref.at[pl.ds(slot, 1)], sem)

    def zero_row_copy(slot):
        return pltpu.make_async_copy(zero_ref.at[pl.ds(0, 1)], xs_ref.at[pl.ds(slot, 1)], sem)

    def zero_blk_copy(b):
        start = pl.multiple_of(b * SLOT_BLK, SLOT_BLK)
        return pltpu.make_async_copy(zero_ref, xs_ref.at[pl.ds(start, SLOT_BLK)], sem)

    @pl.when(i == 0)
    def _():
        zero_ref[...] = jnp.zeros(zero_ref.shape, zero_ref.dtype)

        def per_expert(e, c):
            start = pad_start_ref[e]
            n = pad_len_ref[e]
            lax.fori_loop(0, n, lambda j, cc: (zero_row_copy(start + j).start(), cc)[1], 0)
            lax.fori_loop(0, n, lambda j, cc: (zero_row_copy(start + j).wait(), cc)[1], 0)
            return c

        lax.fori_loop(0, N_EXPERTS, per_expert, 0)
        nu = nused_ref[0]
        lax.fori_loop(nu, n_blocks, lambda b, cc: (zero_blk_copy(b).start(), cc)[1], 0)
        lax.fori_loop(nu, n_blocks, lambda b, cc: (zero_blk_copy(b).wait(), cc)[1], 0)

    def issue(r, c):
        base = (i * tb + r) * TOP_K
        for k in range(TOP_K):
            row_copy(r, dest_ref[base + k]).start()
        return c

    def drain(r, c):
        for k in range(TOP_K):
            row_copy(0, 0).wait()
        return c

    lax.fori_loop(0, tb, issue, 0)
    lax.fori_loop(0, tb, drain, 0)


def _dispatch(dest_flat, pad_start, pad_len, nused, xp, n_slots, tb=256):
    T, W = xp.shape
    return pl.pallas_call(
        _dispatch_kernel,
        grid_spec=pltpu.PrefetchScalarGridSpec(
            num_scalar_prefetch=4,
            grid=(T // tb,),
            in_specs=[pl.BlockSpec((tb, W), lambda i, *_: (i, 0))],
            out_specs=pl.BlockSpec(memory_space=pl.ANY),
            scratch_shapes=[pltpu.VMEM((SLOT_BLK, W), xp.dtype),
                            pltpu.SemaphoreType.DMA(())],
        ),
        out_shape=jax.ShapeDtypeStruct((n_slots, W), xp.dtype),
        compiler_params=_cparams(("arbitrary",)),
        name="dispatch",
    )(dest_flat, pad_start, pad_len, nused, xp)


def _unpack_pair(xw):
    xa = pltpu.unpack_elementwise(xw, index=0, packed_dtype=BF16, unpacked_dtype=F32)
    xb = pltpu.unpack_elementwise(xw, index=1, packed_dtype=BF16, unpacked_dtype=F32)
    return xa.astype(BF16), xb.astype(BF16)


def _moe_kernel(item_e_ref, item_start_ref, item_nsub_ref, nused_ref,
                xs_ref, wgu_ref, wd_ref, bgu_ref, bd_ref, ys_ref,
                x_ref, h_ref, gu_ref, gub_ref, d_ref, db_ref, stage_ref, pend_ref,
                sem_x, sem_gu, sem_d, sem_o):
    i = pl.program_id(0)
    tf = gu_ref.shape[-1]
    tn = d_ref.shape[-1]
    half = gu_ref.shape[2] // 2
    nf = EXPERT_FF // tf
    nn = wd_ref.shape[-1] // tn
    n_blocks = ys_ref.shape[0] // MOE_SUB

    e = item_e_ref[i]
    start = pl.multiple_of(item_start_ref[i], MOE_SUB)
    nsub = item_nsub_ref[i]
    nxt_e = item_e_ref[i + 1]
    nxt_start = pl.multiple_of(item_start_ref[i + 1], MOE_SUB)
    nxt_nsub = item_nsub_ref[i + 1]

    def gu_copy(ee, f, slot, part):
        return pltpu.make_async_copy(wgu_ref.at[0, ee, :, pl.ds(part * EXPERT_FF + f * tf, tf)],
                                     gu_ref.at[slot, part], sem_gu.at[slot])

    def d_copy(ee, n, slot):
        return pltpu.make_async_copy(wd_ref.at[0, ee, :, pl.ds(n * tn, tn)], d_ref.at[slot],
                                     sem_d.at[slot])

    def x_copy(row0, r):
        off = pl.multiple_of(r * MOE_SUB, MOE_SUB)
        return pltpu.make_async_copy(xs_ref.at[pl.ds(row0 + off, MOE_SUB)],
                                     x_ref.at[pl.ds(off, MOE_SUB)], sem_x)

    def out_copy(row0, nrows, n, slot):
        return pltpu.make_async_copy(stage_ref.at[slot, pl.ds(0, nrows)],
                                     ys_ref.at[pl.ds(row0, nrows), pl.ds(n * tn, tn)],
                                     sem_o.at[slot])

    def wait_out(slot):
        for nsubs in (1, 2):
            @pl.when(pend_ref[slot] == nsubs)
            def _(nsubs=nsubs):
                out_copy(start, nsubs * MOE_SUB, 0, slot).wait()

    def for_row_blocks(body):
        npair = nsub // 2

        def pair(p, c):
            body(p, pl.multiple_of(p * 2 * MOE_SUB, 2 * MOE_SUB), 2 * MOE_SUB)
            return c

        lax.fori_loop(0, npair, pair, 0)

        @pl.when(nsub % 2 == 1)
        def _():
            body(npair, pl.multiple_of((nsub - 1) * MOE_SUB, MOE_SUB), MOE_SUB)

    def start_gu(ee, f, slot):
        gu_copy(ee, f, slot, 0).start()
        gu_copy(ee, f, slot, 1).start()

    def load_rows(row0, count):
        lax.fori_loop(0, count, lambda r, c: (x_copy(row0, r).start(), c)[1], 0)

    @pl.when(i == 0)
    def _():
        pend_ref[0] = 0
        pend_ref[1] = 0
        stage_ref[0] = jnp.zeros(stage_ref.shape[1:], F32)

        def zero_block(b, c):
            row0 = pl.multiple_of(b * MOE_SUB, MOE_SUB)
            for n in range(nn):
                out_copy(row0, MOE_SUB, n, 0).start()
            for n in range(nn):
                out_copy(row0, MOE_SUB, n, 0).wait()
            return c

        lax.fori_loop(nused_ref[0], n_blocks, zero_block, 0)

        @pl.when(nsub > 0)
        def _():
            load_rows(start, nsub)
            start_gu(e, 0, 0)

    @pl.when(nsub > 0)
    def _():
        lax.fori_loop(0, nsub, lambda r, c: (x_copy(start, r).wait(), c)[1], 0)

        for f in range(nf):
            slot = f % 2
            gu_copy(e, f, slot, 0).wait()
            gu_copy(e, f, slot, 1).wait()
            if f + 1 < nf:
                start_gu(e, f + 1, 1 - slot)
            else:
                d_copy(e, 0, 0).start()
            gub_ref[:, :tf] = gu_ref[slot, 0].astype(BF16)
            gub_ref[:, tf:] = gu_ref[slot, 1].astype(BF16)
            bias_g = bgu_ref[pl.ds(e, 1), f * tf:(f + 1) * tf]
            bias_u = bgu_ref[pl.ds(e, 1), EXPERT_FF + f * tf:EXPERT_FF + (f + 1) * tf]

            def up_rows(blk, off, nrows, f=f, bias_g=bias_g, bias_u=bias_u):
                rows = pl.ds(off, nrows)
                xa, xb = _unpack_pair(x_ref[rows, :])
                gu = (jnp.dot(xa, gub_ref[:half], preferred_element_type=F32)
                      + jnp.dot(xb, gub_ref[half:], preferred_element_type=F32))
                gate = jnp.minimum(gu[:, :tf] + bias_g, SWIGLU_LIMIT)
                up = jnp.clip(gu[:, tf:] + bias_u, -SWIGLU_LIMIT, SWIGLU_LIMIT)
                glu = gate * _sigmoid(SWIGLU_ALPHA * gate)
                h_ref[rows, f * tf:(f + 1) * tf] = ((up + 1.0) * glu).astype(BF16)

            for_row_blocks(up_rows)

        @pl.when(nxt_nsub > 0)
        def _():
            load_rows(nxt_start, nxt_nsub)

        for n in range(nn):
            slot = n % 2
            d_copy(e, n, slot).wait()
            if n + 1 < nn:
                d_copy(e, n + 1, 1 - slot).start()
            else:
                @pl.when(nxt_nsub > 0)
                def _():
                    start_gu(nxt_e, 0, 0)
            db_ref[...] = d_ref[slot].astype(BF16)
            bias_d = bd_ref[pl.ds(e, 1), n * tn:(n + 1) * tn]

            def down_rows(blk, off, nrows, n=n, bias_d=bias_d):
                os = blk % 2
                wait_out(os)
                stage_ref[os, pl.ds(0, nrows)] = (
                    jnp.dot(h_ref[pl.ds(off, nrows), :], db_ref[...], preferred_element_type=F32)
                    + bias_d)
                out_copy(start + off, nrows, n, os).start()
                pend_ref[os] = nrows // MOE_SUB

            for_row_blocks(down_rows)

        for os in range(2):
            wait_out(os)
            pend_ref[os] = 0


def _moe(item_e, item_start, item_nsub, nused, xs, w_gate_up, b_gate_up, w_down, b_down,
         tf=256, tn=512):
    n_slots, W = xs.shape
    D = 2 * W
    n_items = item_e.shape[0] - 1
    rb = MOE_ITEM_SUBS * MOE_SUB
    any_spec = pl.BlockSpec(memory_space=pl.ANY)
    return pl.pallas_call(
        _moe_kernel,
        grid_spec=pltpu.PrefetchScalarGridSpec(
            num_scalar_prefetch=4,
            grid=(n_items,),
            in_specs=[any_spec, any_spec, any_spec,
                      pl.BlockSpec(b_gate_up.shape, lambda i, *_: (0, 0)),
                      pl.BlockSpec(b_down.shape, lambda i, *_: (0, 0))],
            out_specs=any_spec,
            scratch_shapes=[pltpu.VMEM((rb, W), xs.dtype),
                            pltpu.VMEM((rb, EXPERT_FF), BF16),
                            pltpu.VMEM((2, 2, D, tf), F32),
                            pltpu.VMEM((D, 2 * tf), BF16),
                            pltpu.VMEM((2, EXPERT_FF, tn), F32),
                            pltpu.VMEM((EXPERT_FF, tn), BF16),
                            pltpu.VMEM((2, 2 * MOE_SUB, tn), F32),
                            pltpu.SMEM((2,), I32),
                            pltpu.SemaphoreType.DMA(()),
                            pltpu.SemaphoreType.DMA((2,)),
                            pltpu.SemaphoreType.DMA((2,)),
                            pltpu.SemaphoreType.DMA((2,))],
        ),
        out_shape=jax.ShapeDtypeStruct((n_slots, D), F32),
        compiler_params=_cparams(("arbitrary",)),
        name="moe",
    )(item_e, item_start, item_nsub, nused, xs, w_gate_up, w_down, b_gate_up, b_down)


def _combine_kernel(dest_ref, h_ref, gate_ref, ys_ref, nw_ref, o_ref, buf_ref, sem):
    tb = h_ref.shape[0]
    i = pl.program_id(0)

    def row_copy(slot, k, r):
        return pltpu.make_async_copy(ys_ref.at[pl.ds(slot, 1)], buf_ref.at[k, pl.ds(r, 1)], sem)

    def issue(r, c):
        base = (i * tb + r) * TOP_K
        for k in range(TOP_K):
            row_copy(dest_ref[base + k], k, r).start()
        return c

    def drain(r, c):
        for k in range(TOP_K):
            row_copy(0, k, r).wait()
        return c

    lax.fori_loop(0, tb, issue, 0)
    lax.fori_loop(0, tb, drain, 0)

    gates = gate_ref[...]
    h = h_ref[...]
    for k in range(TOP_K):
        h = h + buf_ref[k] * gates[:, k:k + 1]
    ms = jnp.mean(h * h, axis=-1, keepdims=True)
    o_ref[...] = h * lax.rsqrt(ms + EPS) * nw_ref[...]


def _combine(dest_flat, h1, gates, ys, nw, tb=128):
    T, D = h1.shape
    return pl.pallas_call(
        _combine_kernel,
        grid_spec=pltpu.PrefetchScalarGridSpec(
            num_scalar_prefetch=1,
            grid=(T // tb,),
            in_specs=[pl.BlockSpec((tb, D), lambda i, *_: (i, 0)),
                      pl.BlockSpec((tb, LANE), lambda i, *_: (i, 0)),
                      pl.BlockSpec(memory_space=pl.ANY),
                      pl.BlockSpec((1, D), lambda i, *_: (0, 0))],
            out_specs=pl.BlockSpec((tb, D), lambda i, *_: (i, 0)),
            scratch_shapes=[pltpu.VMEM((TOP_K, tb, D), F32),
                            pltpu.SemaphoreType.DMA(())],
        ),
        out_shape=jax.ShapeDtypeStruct((T, D), F32),
        compiler_params=_cparams(("arbitrary",)),
        name="combine",
    )(dest_flat, h1, gates, ys, nw)


def _layer(h, norm1_w, w_in, conv_a_w, conv_a_norm_w, dn_conv_w, dn_A_log, dn_dt_bias, dn_norm_w,
           w_out, norm2_w, w_router, b_router, w_gate_up, b_gate_up, w_down, b_down, l):
    T, D = h.shape
    main_cols = 3 * CONV_W + 4 * DN_W
    w_in_t = w_in[l].T
    xn, gc, beta = _norm1(h, norm1_w[l][None], w_in_t, main_cols, dn_A_log[l][None],
                          dn_dt_bias[l][None])
    proj = _in_proj(xn, w_in_t, main_cols)
    ya = _conv_mix(proj, conv_a_w[l], conv_a_norm_w[l][None])
    gct = gc.T.reshape(N_HEADS, T // CHUNK, CHUNK)
    yb = _gdn(proj, gc, beta, gct, dn_conv_w[l], dn_norm_w[l][None])
    h1 = _out_proj(ya, yb, w_out[l], h)

    xp, idx, gates, rank, counts = _router(h1, norm2_w[l][None], w_router[l].T, b_router[l][None])
    top_idx = idx[:, :TOP_K]
    counts = counts[0].astype(I32)
    padded = ((counts + SLOT_BLK - 1) // SLOT_BLK) * SLOT_BLK
    pend = jnp.cumsum(padded)
    pstart = pend - padded
    dest_flat = (pstart[top_idx] + rank[:, :TOP_K]).reshape(-1)
    n_slots = T * TOP_K + N_EXPERTS * SLOT_BLK
    nb = n_slots // SLOT_BLK
    nused = (pend[-1] // SLOT_BLK).reshape(1)

    nsub_e = padded // SLOT_BLK
    items_e = (nsub_e + MOE_ITEM_SUBS - 1) // MOE_ITEM_SUBS
    item_end = jnp.cumsum(items_e)
    item_first = item_end - items_e
    n_items = N_EXPERTS + nb // MOE_ITEM_SUBS
    item = jnp.arange(n_items + 1, dtype=I32)
    item_e = jnp.minimum(jnp.sum((item_end[None, :] <= item[:, None]).astype(I32), axis=1),
                         N_EXPERTS - 1)
    item_j = item - item_first[item_e]
    item_valid = item < item_end[-1]
    item_start = jnp.where(item_valid, pstart[item_e] + item_j * (MOE_ITEM_SUBS * SLOT_BLK), 0)
    item_nsub = jnp.where(item_valid,
                          jnp.minimum(nsub_e[item_e] - item_j * MOE_ITEM_SUBS, MOE_ITEM_SUBS), 0)

    xs = _dispatch(dest_flat, pstart + counts, padded - counts, nused, xp, n_slots)
    ys = _moe(item_e, item_start.astype(I32), item_nsub.astype(I32), nused, xs,
              w_gate_up[l:l + 1], b_gate_up[l], w_down[l:l + 1], b_down[l])
    return h1, dest_flat, gates, ys


def kernel(x, norm1_w, w_in, conv_a_w, conv_a_norm_w, dn_conv_w, dn_A_log, dn_dt_bias, dn_norm_w,
           w_out, norm2_w, w_router, b_router, w_gate_up, b_gate_up, w_down, b_down, final_norm_w):
    B, S, D = x.shape
    depth = norm1_w.shape[0]
    assert depth == 1, "the final norm is fused into the last layer's combine"
    h = x.reshape(B * S, D)
    h1, dest_flat, gates, ys = _layer(
        h, norm1_w, w_in, conv_a_w, conv_a_norm_w, dn_conv_w, dn_A_log, dn_dt_bias, dn_norm_w,
        w_out, norm2_w, w_router, b_router, w_gate_up, b_gate_up, w_down, b_down, 0)
    out = _combine(dest_flat, h1, gates, ys, final_norm_w[None])
    return out.reshape(B, S, D)
```

```python
import functools

import jax
import jax.numpy as jnp
from jax import lax
from jax.experimental import pallas as pl
from jax.experimental.pallas import tpu as pltpu

F32 = jnp.float32
BF16 = jnp.bfloat16
I32 = jnp.int32

EPS = 1e-6
CHUNK = 64
HEAD_DIM = 128
N_HEADS = 16
CONV_W = 2048
DN_W = 2048
N_EXPERTS = 32
TOP_K = 4
EXPERT_FF = 2048
SWIGLU_LIMIT = 7.0
SWIGLU_ALPHA = 1.702

LANE = 128
VMEM_LIMIT = 56 * 1024 * 1024
SLOT_BLK = 128
MOE_SUB = SLOT_BLK
MOE_BLOCK_UNITS = 4
MOE_ITEM_SUBS = 10
MOE_DMA_SPLIT = 4
DMA_ISSUE_UNROLL = 4


def _cparams(sem):
    return pltpu.CompilerParams(dimension_semantics=sem, vmem_limit_bytes=VMEM_LIMIT)


def _mm(a, b):
    return jnp.dot(a.astype(BF16), b.astype(BF16), preferred_element_type=F32)


def _dot_nt(a, b):
    return lax.dot_general(a, b, (((1,), (1,)), ((), ())), preferred_element_type=F32)


def _mm_nt(a, b):
    return _dot_nt(a.astype(BF16), b.astype(BF16))


def _mm_tn(a, b):
    return lax.dot_general(a.astype(BF16), b.astype(BF16), (((0,), (0,)), ((), ())),
                           preferred_element_type=F32)


def _mm3_nt(x, wt):
    xh = x.astype(BF16)
    xl = (x - xh.astype(F32)).astype(BF16)
    wh = wt.astype(BF16)
    wl = (wt - wh.astype(F32)).astype(BF16)
    return _dot_nt(xh, wh) + _dot_nt(xl, wh) + _dot_nt(xh, wl)


def _sigmoid(x):
    return 1.0 / (1.0 + jnp.exp(-x))


def _norm1_kernel(x_ref, nw_ref, wab_ref, alog_ref, dtb_ref, xn_ref, gc_ref, beta_ref):
    x = x_ref[...]
    ms = jnp.mean(x * x, axis=-1, keepdims=True)
    xn = x * lax.rsqrt(ms + EPS) * nw_ref[...]
    xn_ref[...] = xn.astype(BF16)
    ab = _mm3_nt(xn, wab_ref[...])
    a = ab[:, :N_HEADS] + dtb_ref[...]
    b = ab[:, N_HEADS:]
    softplus = jnp.maximum(a, 0.0) + jnp.log1p(jnp.exp(-jnp.abs(a)))
    g = -jnp.exp(alog_ref[...]) * softplus
    pos = lax.broadcasted_iota(I32, g.shape, 0) % CHUNK
    shift = 1
    while shift < CHUNK:
        g = g + jnp.where(pos >= shift, pltpu.roll(g, shift, 0), 0.0)
        shift *= 2
    gc_ref[...] = g
    beta_ref[...] = _sigmoid(b)


def _norm1(x, nw, w_in_t, ab_row, alog, dtb, bm=256):
    T, D = x.shape
    nab = 2 * N_HEADS
    return pl.pallas_call(
        _norm1_kernel,
        grid=(T // bm,),
        in_specs=[pl.BlockSpec((bm, D), lambda i: (i, 0)),
                  pl.BlockSpec((1, D), lambda i: (0, 0)),
                  pl.BlockSpec((nab, D), lambda i: (ab_row // nab, 0)),
                  pl.BlockSpec((1, N_HEADS), lambda i: (0, 0)),
                  pl.BlockSpec((1, N_HEADS), lambda i: (0, 0))],
        out_specs=[pl.BlockSpec((bm, D), lambda i: (i, 0)),
                   pl.BlockSpec((bm, N_HEADS), lambda i: (i, 0)),
                   pl.BlockSpec((bm, N_HEADS), lambda i: (i, 0))],
        out_shape=[jax.ShapeDtypeStruct((T, D), BF16),
                   jax.ShapeDtypeStruct((T, N_HEADS), F32),
                   jax.ShapeDtypeStruct((T, N_HEADS), F32)],
        compiler_params=_cparams(("arbitrary",)),
        name="norm1",
    )(x, nw, w_in_t, alog, dtb)


def _in_proj_kernel(x_ref, w_ref, o_ref, wb_ref):
    @pl.when(pl.program_id(1) == 0)
    def _():
        wb_ref[...] = w_ref[...].astype(BF16)

    o_ref[...] = _dot_nt(x_ref[...], wb_ref[...])


def _in_proj(xn, w_in_t, n_cols, bm=1024, bn=512):
    T, D = xn.shape
    return pl.pallas_call(
        _in_proj_kernel,
        grid=(n_cols // bn, T // bm),
        in_specs=[pl.BlockSpec((bm, D), lambda j, i: (i, 0)),
                  pl.BlockSpec((bn, D), lambda j, i: (j, 0))],
        out_specs=pl.BlockSpec((bm, bn), lambda j, i: (i, j)),
        out_shape=jax.ShapeDtypeStruct((T, n_cols), F32),
        scratch_shapes=[pltpu.VMEM((bn, D), BF16)],
        compiler_params=_cparams(("arbitrary", "arbitrary")),
        name="in_proj",
    )(xn, w_in_t)


def _conv_mix_kernel(x_ref, b_ref, c_ref, xh_ref, ch_ref, cw_ref, nw_ref, o_ref):
    bm, bc = x_ref.shape
    u = c_ref[...] * x_ref[...]
    halo = jnp.where(pl.program_id(0) > 0, ch_ref[...] * xh_ref[...], 0.0)
    ext = jnp.concatenate([halo, u], axis=0)
    cw = cw_ref[...]
    conv = ext[6:6 + bm] * cw[0:1] + ext[7:7 + bm] * cw[1:2] + u * cw[2:3]
    y = b_ref[...] * conv
    nw = nw_ref[...]
    for g in range(bc // LANE):
        cs = slice(g * LANE, (g + 1) * LANE)
        yg = y[:, cs]
        ms = jnp.mean(yg * yg, axis=-1, keepdims=True)
        o_ref[:, cs] = (yg * lax.rsqrt(ms + EPS) * nw[:, cs]).astype(BF16)


def _conv_mix(proj, conv_w, norm_w, bm=512, bc=512):
    T = proj.shape[0]
    nc = CONV_W // bc
    hb = bm // 8
    halo_map = lambda off: (lambda i, j: (jnp.maximum(i * hb - 1, 0), j + off))
    return pl.pallas_call(
        _conv_mix_kernel,
        grid=(T // bm, nc),
        in_specs=[pl.BlockSpec((bm, bc), lambda i, j: (i, j)),
                  pl.BlockSpec((bm, bc), lambda i, j: (i, j + nc)),
                  pl.BlockSpec((bm, bc), lambda i, j: (i, j + 2 * nc)),
                  pl.BlockSpec((8, bc), halo_map(0)),
                  pl.BlockSpec((8, bc), halo_map(2 * nc)),
                  pl.BlockSpec((3, bc), lambda i, j: (0, j)),
                  pl.BlockSpec((1, bc), lambda i, j: (0, j))],
        out_specs=pl.BlockSpec((bm, bc), lambda i, j: (i, j)),
        out_shape=jax.ShapeDtypeStruct((T, CONV_W), BF16),
        compiler_params=_cparams(("arbitrary", "arbitrary")),
        name="conv_mix",
    )(proj, proj, proj, proj, proj, conv_w, norm_w)


def _gdn_kernel(q_ref, k_ref, v_ref, z_ref, gc_ref, beta_ref, gct_ref, cw_ref, nw_ref,
                y_ref, qs_ref, ks_ref, vs_ref, halo_ref, s_ref):
    tb = q_ref.shape[0]
    t = pl.program_id(0)

    @pl.when(t == 0)
    def _():
        halo_ref[...] = jnp.zeros(halo_ref.shape, F32)
        s_ref[...] = jnp.zeros(s_ref.shape, F32)

    for a, (src, dst) in enumerate(((q_ref, qs_ref), (k_ref, ks_ref), (v_ref, vs_ref))):
        for h in range(N_HEADS):
            cs = slice(h * HEAD_DIM, (h + 1) * HEAD_DIM)
            raw = src[:, cs]
            ext = jnp.concatenate([halo_ref[a, :, cs], raw], axis=0)
            cw = cw_ref[:, a * DN_W + h * HEAD_DIM:a * DN_W + (h + 1) * HEAD_DIM]
            c = (ext[5:5 + tb] * cw[0:1] + ext[6:6 + tb] * cw[1:2]
                 + ext[7:7 + tb] * cw[2:3] + raw * cw[3:4])
            c = c * _sigmoid(c)
            if a < 2:
                c = c * lax.rsqrt(jnp.sum(c * c, axis=-1, keepdims=True) + EPS)
            if a == 0:
                c = c * (HEAD_DIM ** -0.5)
            dst[:, cs] = c
            halo_ref[a, :, cs] = raw[tb - 8:tb]

    ii = lax.broadcasted_iota(I32, (CHUNK, CHUNK), 0)
    jj = lax.broadcasted_iota(I32, (CHUNK, CHUNK), 1)
    eye = (ii == jj).astype(F32)
    nw = nw_ref[...]

    heads = range(N_HEADS)
    cols = [slice(h * HEAD_DIM, (h + 1) * HEAD_DIM) for h in heads]

    def chunk(c, carry):
        rows = pl.ds(pl.multiple_of(c * CHUNK, CHUNK), CHUNK)
        cg = t * (tb // CHUNK) + c
        gcb = gc_ref[rows, :]
        btb = beta_ref[rows, :]
        q = [qs_ref[rows, cs] for cs in cols]
        k = [ks_ref[rows, cs] for cs in cols]
        v = [vs_ref[rows, cs] for cs in cols]
        z = [z_ref[rows, cs] for cs in cols]
        s = [s_ref[h] for h in heads]
        gcol = [gcb[:, h:h + 1] for h in heads]
        bcol = [btb[:, h:h + 1] for h in heads]
        grow = [gct_ref[h, pl.ds(cg, 1), :] for h in heads]
        gl = [g[:, CHUNK - 1:CHUNK] for g in grow]
        eg = [jnp.exp(g) for g in gcol]
        kb = [k[h] * bcol[h] for h in heads]
        a1 = [_mm_nt(jnp.concatenate([kb[h], q[h]], axis=0), k[h]) for h in heads]
        dec = [jnp.where(ii >= jj, jnp.exp(jnp.minimum(gcol[h] - grow[h], 0.0)), 0.0) for h in heads]
        lmat = [jnp.where(ii > jj, a1[h][:CHUNK] * dec[h], 0.0) for h in heads]
        qkd = [a1[h][CHUNK:] * dec[h] for h in heads]
        p = [eye - lmat[h] for h in heads]
        cur = [_mm(lmat[h], lmat[h]) for h in heads]
        power = 2
        while 2 * power < CHUNK:
            r = [_mm(jnp.concatenate([p[h], cur[h]], axis=0), cur[h]) for h in heads]
            p = [p[h] + r[h][:CHUNK] for h in heads]
            cur = [r[h][CHUNK:] for h in heads]
            power *= 2
        p = [p[h] + _mm(p[h], cur[h]) for h in heads]
        uw = [_mm(p[h], jnp.concatenate([v[h] * bcol[h], kb[h] * eg[h]], axis=1)) for h in heads]
        ws = [_mm(jnp.concatenate([uw[h][:, HEAD_DIM:], q[h] * eg[h]], axis=0), s[h]) for h in heads]
        vn = [uw[h][:, :HEAD_DIM] - ws[h][:CHUNK] for h in heads]
        o = [ws[h][CHUNK:] + _mm(qkd[h], vn[h]) for h in heads]
        kt = [k[h] * jnp.exp(gl[h] - gcol[h]) for h in heads]
        s_new = [s[h] * jnp.exp(gl[h]) + _mm_tn(kt[h], vn[h]) for h in heads]
        for h in heads:
            on = o[h] * lax.rsqrt(jnp.mean(o[h] * o[h], axis=-1, keepdims=True) + EPS) * nw
            y_ref[rows, cols[h]] = (on * (z[h] * _sigmoid(z[h]))).astype(BF16)
            s_ref[h] = s_new[h]
        return carry

    lax.fori_loop(0, tb // CHUNK, chunk, 0)


def _gdn(proj, gc, beta, gct, conv_w, norm_w, tb=256):
    T = proj.shape[0]
    qb = (3 * CONV_W) // DN_W
    n_chunks = T // CHUNK
    return pl.pallas_call(
        _gdn_kernel,
        grid=(T // tb,),
        in_specs=[pl.BlockSpec((tb, DN_W), lambda t: (t, qb)),
                  pl.BlockSpec((tb, DN_W), lambda t: (t, qb + 1)),
                  pl.BlockSpec((tb, DN_W), lambda t: (t, qb + 2)),
                  pl.BlockSpec((tb, DN_W), lambda t: (t, qb + 3)),
                  pl.BlockSpec((tb, N_HEADS), lambda t: (t, 0)),
                  pl.BlockSpec((tb, N_HEADS), lambda t: (t, 0)),
                  pl.BlockSpec((N_HEADS, n_chunks, CHUNK), lambda t: (0, 0, 0)),
                  pl.BlockSpec((4, 3 * DN_W), lambda t: (0, 0)),
                  pl.BlockSpec((1, HEAD_DIM), lambda t: (0, 0))],
        out_specs=pl.BlockSpec((tb, DN_W), lambda t: (t, 0)),
        out_shape=jax.ShapeDtypeStruct((T, DN_W), BF16),
        scratch_shapes=[pltpu.VMEM((tb, DN_W), F32),
                        pltpu.VMEM((tb, DN_W), F32),
                        pltpu.VMEM((tb, DN_W), F32),
                        pltpu.VMEM((3, 8, DN_W), F32),
                        pltpu.VMEM((N_HEADS, HEAD_DIM, HEAD_DIM), F32)],
        compiler_params=_cparams(("arbitrary",)),
        name="gdn",
    )(proj, proj, proj, proj, gc, beta, gct, conv_w, norm_w)


def _out_proj_kernel(ya_ref, yb_ref, w_ref, x_ref, o_ref, wb_ref):
    @pl.when(pl.program_id(1) == 0)
    def _():
        wb_ref[...] = w_ref[...].astype(BF16)

    o_ref[...] = (x_ref[...]
                  + jnp.dot(ya_ref[...], wb_ref[:CONV_W], preferred_element_type=F32)
                  + jnp.dot(yb_ref[...], wb_ref[CONV_W:], preferred_element_type=F32))


def _out_proj(ya, yb, w_out, x, bm=1024, bn=512):
    T, D = x.shape
    K = w_out.shape[0]
    return pl.pallas_call(
        _out_proj_kernel,
        grid=(D // bn, T // bm),
        in_specs=[pl.BlockSpec((bm, CONV_W), lambda j, i: (i, 0)),
                  pl.BlockSpec((bm, DN_W), lambda j, i: (i, 0)),
                  pl.BlockSpec((K, bn), lambda j, i: (0, j)),
                  pl.BlockSpec((bm, bn), lambda j, i: (i, j))],
        out_specs=pl.BlockSpec((bm, bn), lambda j, i: (i, j)),
        out_shape=jax.ShapeDtypeStruct((T, D), F32),
        scratch_shapes=[pltpu.VMEM((K, bn), BF16)],
        compiler_params=_cparams(("arbitrary", "arbitrary")),
        name="out_proj",
    )(ya, yb, w_out, x)


def _router_kernel(h_ref, nw_ref, wr_ref, br_ref, xp_ref, idx_ref, gate_ref, rank_ref, cnt_ref,
                   carry_ref):
    bm, D = h_ref.shape

    @pl.when(pl.program_id(0) == 0)
    def _():
        carry_ref[...] = jnp.zeros(carry_ref.shape, F32)

    h = h_ref[...]
    ms = jnp.mean(h * h, axis=-1, keepdims=True)
    xn = h * lax.rsqrt(ms + EPS) * nw_ref[...]
    xp_ref[...] = pltpu.pack_elementwise([xn[:, :D // 2], xn[:, D // 2:]], packed_dtype=BF16)

    logits = _mm3_nt(xn, wr_ref[...]) + br_ref[...]
    lane_e = lax.broadcasted_iota(I32, logits.shape, 1)
    work = logits
    sels, vals, idxs = [], [], []
    for _ in range(TOP_K):
        m = jnp.max(work, axis=-1, keepdims=True)
        idx = jnp.min(jnp.where(work == m, lane_e, N_EXPERTS), axis=-1, keepdims=True)
        sel = lane_e == idx
        work = jnp.where(sel, -jnp.inf, work)
        sels.append(sel)
        vals.append(m)
        idxs.append(idx)
    exps = [jnp.exp(v - vals[0]) for v in vals]
    denom = exps[0] + exps[1] + exps[2] + exps[3]
    gates = [e / denom for e in exps]

    onehot = jnp.zeros(logits.shape, F32)
    for sel in sels:
        onehot = onehot + sel.astype(F32)
    ri = lax.broadcasted_iota(I32, (bm, bm), 0)
    ci = lax.broadcasted_iota(I32, (bm, bm), 1)
    tri = (ri > ci).astype(BF16)
    carry = carry_ref[...]
    before = jnp.dot(tri, onehot.astype(BF16), preferred_element_type=F32) + carry
    ranks = [jnp.sum(jnp.where(sel, before, 0.0), axis=-1, keepdims=True).astype(I32)
             for sel in sels]
    new_carry = carry + jnp.sum(onehot, axis=0, keepdims=True)
    carry_ref[...] = new_carry
    cnt_ref[...] = new_carry

    lane = lax.broadcasted_iota(I32, (bm, LANE), 1)

    def spread(cols):
        return jnp.where(lane == 0, cols[0],
                         jnp.where(lane == 1, cols[1], jnp.where(lane == 2, cols[2], cols[3])))

    idx_ref[...] = spread(idxs)
    gate_ref[...] = spread(gates)
    rank_ref[...] = spread(ranks)


def _router(h1, nw, w_router_t, b_router, bm=256):
    T, D = h1.shape
    return pl.pallas_call(
        _router_kernel,
        grid=(T // bm,),
        in_specs=[pl.BlockSpec((bm, D), lambda i: (i, 0)),
                  pl.BlockSpec((1, D), lambda i: (0, 0)),
                  pl.BlockSpec((N_EXPERTS, D), lambda i: (0, 0)),
                  pl.BlockSpec((1, N_EXPERTS), lambda i: (0, 0))],
        out_specs=[pl.BlockSpec((bm, D // 2), lambda i: (i, 0)),
                   pl.BlockSpec((bm, LANE), lambda i: (i, 0)),
                   pl.BlockSpec((bm, LANE), lambda i: (i, 0)),
                   pl.BlockSpec((bm, LANE), lambda i: (i, 0)),
                   pl.BlockSpec((1, N_EXPERTS), lambda i: (0, 0))],
        out_shape=[jax.ShapeDtypeStruct((T, D // 2), jnp.uint32),
                   jax.ShapeDtypeStruct((T, LANE), I32),
                   jax.ShapeDtypeStruct((T, LANE), F32),
                   jax.ShapeDtypeStruct((T, LANE), I32),
                   jax.ShapeDtypeStruct((1, N_EXPERTS), F32)],
        scratch_shapes=[pltpu.VMEM((1, N_EXPERTS), F32)],
        compiler_params=_cparams(("arbitrary",)),
        name="router",
    )(h1, nw, w_router_t, b_router)


def _dispatch_kernel(dest_ref, pad_start_ref, pad_len_ref, nused_ref, x_ref, xs_ref, zero_ref, sem):
    tb = x_ref.shape[0]
    i = pl.program_id(0)
    n_blocks = xs_ref.shape[0] // SLOT_BLK

    def row_copy(r, slot):
        return pltpu.make_async_copy(x_ref.at[pl.ds(r, 1)], xs_ref.at[pl.ds(slot, 1)], sem)

    def zero_row_copy(slot):
        return pltpu.make_async_copy(zero_ref.at[pl.ds(0, 1)], xs_ref.at[pl.ds(slot, 1)], sem)

    def zero_blk_copy(b):
        start = pl.multiple_of(b * SLOT_BLK, SLOT_BLK)
        return pltpu.make_async_copy(zero_ref, xs_ref.at[pl.ds(start, SLOT_BLK)], sem)

    @pl.when(i == 0)
    def _():
        zero_ref[...] = jnp.zeros(zero_ref.shape, zero_ref.dtype)

        def per_expert(e, c):
            start = pad_start_ref[e]
            n = pad_len_ref[e]
            lax.fori_loop(0, n, lambda j, cc: (zero_row_copy(start + j).start(), cc)[1], 0)
            lax.fori_loop(0, n, lambda j, cc: (zero_row_copy(start + j).wait(), cc)[1], 0)
            return c

        lax.fori_loop(0, N_EXPERTS, per_expert, 0)
        nu = nused_ref[0]
        lax.fori_loop(nu, n_blocks, lambda b, cc: (zero_blk_copy(b).start(), cc)[1], 0)
        lax.fori_loop(nu, n_blocks, lambda b, cc: (zero_blk_copy(b).wait(), cc)[1], 0)

    def issue(r, c):
        base = (i * tb + r) * TOP_K
        for k in range(TOP_K):
            row_copy(r, dest_ref[base + k]).start()
        return c

    def drain(r, c):
        for k in range(TOP_K):
            row_copy(0, 0).wait()
        return c

    lax.fori_loop(0, tb, issue, 0, unroll=DMA_ISSUE_UNROLL)
    lax.fori_loop(0, tb, drain, 0, unroll=DMA_ISSUE_UNROLL)


def _dispatch(dest_flat, pad_start, pad_len, nused, xp, n_slots, tb=256):
    T, W = xp.shape
    return pl.pallas_call(
        _dispatch_kernel,
        grid_spec=pltpu.PrefetchScalarGridSpec(
            num_scalar_prefetch=4,
            grid=(T // tb,),
            in_specs=[pl.BlockSpec((tb, W), lambda i, *_: (i, 0))],
            out_specs=pl.BlockSpec(memory_space=pl.ANY),
            scratch_shapes=[pltpu.VMEM((SLOT_BLK, W), xp.dtype),
                            pltpu.SemaphoreType.DMA(())],
        ),
        out_shape=jax.ShapeDtypeStruct((n_slots, W), xp.dtype),
        compiler_params=_cparams(("arbitrary",)),
        name="dispatch",
    )(dest_flat, pad_start, pad_len, nused, xp)


def _unpack_pair(xw):
    xa = pltpu.unpack_elementwise(xw, index=0, packed_dtype=BF16, unpacked_dtype=F32)
    xb = pltpu.unpack_elementwise(xw, index=1, packed_dtype=BF16, unpacked_dtype=F32)
    return xa.astype(BF16), xb.astype(BF16)


def _moe_kernel(item_e_ref, item_start_ref, item_nsub_ref, nused_ref,
                xs_ref, wgu_ref, wd_ref, bgu_ref, bd_ref, ys_ref,
                x_ref, h_ref, gu_ref, gub_ref, d_ref, db_ref, stage_ref, pend_ref,
                sem_x, sem_gu, sem_d, sem_o):
    i = pl.program_id(0)
    tf = gu_ref.shape[-1]
    tn = d_ref.shape[-1]
    half = gu_ref.shape[2] // 2
    nf = EXPERT_FF // tf
    nn = wd_ref.shape[-1] // tn
    n_blocks = ys_ref.shape[0] // MOE_SUB

    e = item_e_ref[i]
    start = pl.multiple_of(item_start_ref[i], MOE_SUB)
    nsub = item_nsub_ref[i]
    nxt_e = item_e_ref[i + 1]
    nxt_start = pl.multiple_of(item_start_ref[i + 1], MOE_SUB)
    nxt_nsub = item_nsub_ref[i + 1]

    gu_rows = gu_ref.shape[2] // MOE_DMA_SPLIT
    d_rows = d_ref.shape[1] // MOE_DMA_SPLIT

    def gu_copies(ee, f, slot):
        return [pltpu.make_async_copy(
            wgu_ref.at[0, ee, pl.ds(s * gu_rows, gu_rows), pl.ds(part * EXPERT_FF + f * tf, tf)],
            gu_ref.at[slot, part, pl.ds(s * gu_rows, gu_rows)], sem_gu.at[slot])
            for part in range(2) for s in range(MOE_DMA_SPLIT)]

    def d_copies(ee, n, slot):
        return [pltpu.make_async_copy(
            wd_ref.at[0, ee, pl.ds(s * d_rows, d_rows), pl.ds(n * tn, tn)],
            d_ref.at[slot, pl.ds(s * d_rows, d_rows)], sem_d.at[slot])
            for s in range(MOE_DMA_SPLIT)]

    def x_copy(row0, r):
        off = pl.multiple_of(r * MOE_SUB, MOE_SUB)
        return pltpu.make_async_copy(xs_ref.at[pl.ds(row0 + off, MOE_SUB)],
                                     x_ref.at[pl.ds(off, MOE_SUB)], sem_x)

    def out_copy(row0, nrows, n, slot):
        return pltpu.make_async_copy(stage_ref.at[slot, pl.ds(0, nrows)],
                                     ys_ref.at[pl.ds(row0, nrows), pl.ds(n * tn, tn)],
                                     sem_o.at[slot])

    def wait_out(slot):
        for units in (1, 2, MOE_BLOCK_UNITS):
            @pl.when(pend_ref[slot] == units)
            def _(units=units):
                out_copy(start, units * MOE_SUB, 0, slot).wait()

    def for_row_blocks(body):
        nfull = nsub // MOE_BLOCK_UNITS
        rem = nsub % MOE_BLOCK_UNITS
        full_rows = MOE_BLOCK_UNITS * MOE_SUB

        def full(p, c):
            body(p, pl.multiple_of(p * full_rows, full_rows), full_rows)
            return c

        lax.fori_loop(0, nfull, full, 0)
        has2 = (rem // 2) % 2

        @pl.when(has2 == 1)
        def _():
            body(nfull, pl.multiple_of(nfull * full_rows, full_rows), 2 * MOE_SUB)

        @pl.when(rem % 2 == 1)
        def _():
            body(nfull + has2,
                 pl.multiple_of(nfull * full_rows + has2 * 2 * MOE_SUB, MOE_SUB), MOE_SUB)

    def start_gu(ee, f, slot):
        for cp in gu_copies(ee, f, slot):
            cp.start()

    def start_d(ee, n, slot):
        for cp in d_copies(ee, n, slot):
            cp.start()

    def load_rows(row0, count):
        lax.fori_loop(0, count, lambda r, c: (x_copy(row0, r).start(), c)[1], 0)

    @pl.when(i == 0)
    def _():
        pend_ref[0] = 0
        pend_ref[1] = 0
        stage_ref[0] = jnp.zeros(stage_ref.shape[1:], F32)

        def zero_block(b, c):
            row0 = pl.multiple_of(b * MOE_SUB, MOE_SUB)
            for n in range(nn):
                out_copy(row0, MOE_SUB, n, 0).start()
            for n in range(nn):
                out_copy(row0, MOE_SUB, n, 0).wait()
            return c

        lax.fori_loop(nused_ref[0], n_blocks, zero_block, 0)

        @pl.when(nsub > 0)
        def _():
            load_rows(start, nsub)
            start_gu(e, 0, 0)

    @pl.when(nsub > 0)
    def _():
        lax.fori_loop(0, nsub, lambda r, c: (x_copy(start, r).wait(), c)[1], 0)

        for f in range(nf):
            slot = f % 2
            for cp in gu_copies(e, f, slot):
                cp.wait()
            if f + 1 < nf:
                start_gu(e, f + 1, 1 - slot)
            else:
                start_d(e, 0, 0)
            gub_ref[:, :tf] = gu_ref[slot, 0].astype(BF16)
            gub_ref[:, tf:] = gu_ref[slot, 1].astype(BF16)
            bias_g = bgu_ref[pl.ds(e, 1), f * tf:(f + 1) * tf]
            bias_u = bgu_ref[pl.ds(e, 1), EXPERT_FF + f * tf:EXPERT_FF + (f + 1) * tf]

            def up_rows(blk, off, nrows, f=f, bias_g=bias_g, bias_u=bias_u):
                rows = pl.ds(off, nrows)
                xa, xb = _unpack_pair(x_ref[rows, :])
                gu = (jnp.dot(xa, gub_ref[:half], preferred_element_type=F32)
                      + jnp.dot(xb, gub_ref[half:], preferred_element_type=F32))
                gate = jnp.minimum(gu[:, :tf] + bias_g, SWIGLU_LIMIT)
                up = jnp.clip(gu[:, tf:] + bias_u, -SWIGLU_LIMIT, SWIGLU_LIMIT)
                glu = gate * _sigmoid(SWIGLU_ALPHA * gate)
                h_ref[rows, f * tf:(f + 1) * tf] = ((up + 1.0) * glu).astype(BF16)

            for_row_blocks(up_rows)

        @pl.when(nxt_nsub > 0)
        def _():
            load_rows(nxt_start, nxt_nsub)

        for n in range(nn):
            slot = n % 2
            for cp in d_copies(e, n, slot):
                cp.wait()
            if n + 1 < nn:
                start_d(e, n + 1, 1 - slot)
            else:
                @pl.when(nxt_nsub > 0)
                def _():
                    start_gu(nxt_e, 0, 0)
            db_ref[...] = d_ref[slot].astype(BF16)
            bias_d = bd_ref[pl.ds(e, 1), n * tn:(n + 1) * tn]

            def down_rows(blk, off, nrows, n=n, bias_d=bias_d):
                os = blk % 2
                wait_out(os)
                stage_ref[os, pl.ds(0, nrows)] = (
                    jnp.dot(h_ref[pl.ds(off, nrows), :], db_ref[...], preferred_element_type=F32)
                    + bias_d)
                out_copy(start + off, nrows, n, os).start()
                pend_ref[os] = nrows // MOE_SUB

            for_row_blocks(down_rows)

        for os in range(2):
            wait_out(os)
            pend_ref[os] = 0


def _moe(item_e, item_start, item_nsub, nused, xs, w_gate_up, b_gate_up, w_down, b_down,
         tf=256, tn=512):
    n_slots, W = xs.shape
    D = 2 * W
    n_items = item_e.shape[0] - 1
    rb = MOE_ITEM_SUBS * MOE_SUB
    any_spec = pl.BlockSpec(memory_space=pl.ANY)
    return pl.pallas_call(
        _moe_kernel,
        grid_spec=pltpu.PrefetchScalarGridSpec(
            num_scalar_prefetch=4,
            grid=(n_items,),
            in_specs=[any_spec, any_spec, any_spec,
                      pl.BlockSpec(b_gate_up.shape, lambda i, *_: (0, 0)),
                      pl.BlockSpec(b_down.shape, lambda i, *_: (0, 0))],
            out_specs=any_spec,
            scratch_shapes=[pltpu.VMEM((rb, W), xs.dtype),
                            pltpu.VMEM((rb, EXPERT_FF), BF16),
                            pltpu.VMEM((2, 2, D, tf), F32),
                            pltpu.VMEM((D, 2 * tf), BF16),
                            pltpu.VMEM((2, EXPERT_FF, tn), F32),
                            pltpu.VMEM((EXPERT_FF, tn), BF16),
                            pltpu.VMEM((2, MOE_BLOCK_UNITS * MOE_SUB, tn), F32),
                            pltpu.SMEM((2,), I32),
                            pltpu.SemaphoreType.DMA(()),
                            pltpu.SemaphoreType.DMA((2,)),
                            pltpu.SemaphoreType.DMA((2,)),
                            pltpu.SemaphoreType.DMA((2,))],
        ),
        out_shape=jax.ShapeDtypeStruct((n_slots, D), F32),
        compiler_params=_cparams(("arbitrary",)),
        name="moe",
    )(item_e, item_start, item_nsub, nused, xs, w_gate_up, w_down, b_gate_up, b_down)


def _combine_kernel(dest_ref, h_ref, gate_ref, ys_ref, nw_ref, o_ref, buf_ref, sem):
    tb = h_ref.shape[0]
    i = pl.program_id(0)
    n_steps = pl.num_programs(0)

    def row_copy(slot, buf, k, r):
        return pltpu.make_async_copy(ys_ref.at[pl.ds(slot, 1)], buf_ref.at[buf, k, pl.ds(r, 1)],
                                     sem.at[buf])

    def gather(step, buf):
        def issue(r, c):
            base = (step * tb + r) * TOP_K
            for k in range(TOP_K):
                row_copy(dest_ref[base + k], buf, k, r).start()
            return c

        lax.fori_loop(0, tb, issue, 0, unroll=DMA_ISSUE_UNROLL)

    @pl.when(i == 0)
    def _():
        gather(0, 0)

    cur = i % 2

    @pl.when(i + 1 < n_steps)
    def _():
        gather(i + 1, 1 - cur)

    def drain(r, c):
        for k in range(TOP_K):
            row_copy(0, cur, k, r).wait()
        return c

    lax.fori_loop(0, tb, drain, 0, unroll=DMA_ISSUE_UNROLL)

    gates = gate_ref[...]
    h = h_ref[...]
    for k in range(TOP_K):
        h = h + buf_ref[cur, k] * gates[:, k:k + 1]
    ms = jnp.mean(h * h, axis=-1, keepdims=True)
    o_ref[...] = h * lax.rsqrt(ms + EPS) * nw_ref[...]


def _combine(dest_flat, h1, gates, ys, nw, tb=128):
    T, D = h1.shape
    return pl.pallas_call(
        _combine_kernel,
        grid_spec=pltpu.PrefetchScalarGridSpec(
            num_scalar_prefetch=1,
            grid=(T // tb,),
            in_specs=[pl.BlockSpec((tb, D), lambda i, *_: (i, 0)),
                      pl.BlockSpec((tb, LANE), lambda i, *_: (i, 0)),
                      pl.BlockSpec(memory_space=pl.ANY),
                      pl.BlockSpec((1, D), lambda i, *_: (0, 0))],
            out_specs=pl.BlockSpec((tb, D), lambda i, *_: (i, 0)),
            scratch_shapes=[pltpu.VMEM((2, TOP_K, tb, D), F32),
                            pltpu.SemaphoreType.DMA((2,))],
        ),
        out_shape=jax.ShapeDtypeStruct((T, D), F32),
        compiler_params=_cparams(("arbitrary",)),
        name="combine",
    )(dest_flat, h1, gates, ys, nw)


def _layer(h, norm1_w, w_in, conv_a_w, conv_a_norm_w, dn_conv_w, dn_A_log, dn_dt_bias, dn_norm_w,
           w_out, norm2_w, w_router, b_router, w_gate_up, b_gate_up, w_down, b_down, l):
    T, D = h.shape
    main_cols = 3 * CONV_W + 4 * DN_W
    w_in_t = w_in[l].T
    xn, gc, beta = _norm1(h, norm1_w[l][None], w_in_t, main_cols, dn_A_log[l][None],
                          dn_dt_bias[l][None])
    proj = _in_proj(xn, w_in_t, main_cols)
    ya = _conv_mix(proj, conv_a_w[l], conv_a_norm_w[l][None])
    gct = gc.T.reshape(N_HEADS, T // CHUNK, CHUNK)
    yb = _gdn(proj, gc, beta, gct, dn_conv_w[l], dn_norm_w[l][None])
    h1 = _out_proj(ya, yb, w_out[l], h)

    xp, idx, gates, rank, counts = _router(h1, norm2_w[l][None], w_router[l].T, b_router[l][None])
    top_idx = idx[:, :TOP_K]
    counts = counts[0].astype(I32)
    padded = ((counts + SLOT_BLK - 1) // SLOT_BLK) * SLOT_BLK
    pend = jnp.cumsum(padded)
    pstart = pend - padded
    dest_flat = (pstart[top_idx] + rank[:, :TOP_K]).reshape(-1)
    n_slots = T * TOP_K + N_EXPERTS * SLOT_BLK
    nb = n_slots // SLOT_BLK
    nused = (pend[-1] // SLOT_BLK).reshape(1)

    nsub_e = padded // SLOT_BLK
    items_e = (nsub_e + MOE_ITEM_SUBS - 1) // MOE_ITEM_SUBS
    item_end = jnp.cumsum(items_e)
    item_first = item_end - items_e
    n_items = N_EXPERTS + nb // MOE_ITEM_SUBS
    item = jnp.arange(n_items + 1, dtype=I32)
    item_e = jnp.minimum(jnp.sum((item_end[None, :] <= item[:, None]).astype(I32), axis=1),
                         N_EXPERTS - 1)
    item_j = item - item_first[item_e]
    item_valid = item < item_end[-1]
    item_start = jnp.where(item_valid, pstart[item_e] + item_j * (MOE_ITEM_SUBS * SLOT_BLK), 0)
    item_nsub = jnp.where(item_valid,
                          jnp.minimum(nsub_e[item_e] - item_j * MOE_ITEM_SUBS, MOE_ITEM_SUBS), 0)

    xs = _dispatch(dest_flat, pstart + counts, padded - counts, nused, xp, n_slots)
    ys = _moe(item_e, item_start.astype(I32), item_nsub.astype(I32), nused, xs,
              w_gate_up[l:l + 1], b_gate_up[l], w_down[l:l + 1], b_down[l])
    return h1, dest_flat, gates, ys


def kernel(x, norm1_w, w_in, conv_a_w, conv_a_norm_w, dn_conv_w, dn_A_log, dn_dt_bias, dn_norm_w,
           w_out, norm2_w, w_router, b_router, w_gate_up, b_gate_up, w_down, b_down, final_norm_w):
    B, S, D = x.shape
    depth = norm1_w.shape[0]
    assert depth == 1, "the final norm is fused into the last layer's combine"
    h = x.reshape(B * S, D)
    h1, dest_flat, gates, ys = _layer(
        h, norm1_w, w_in, conv_a_w, conv_a_norm_w, dn_conv_w, dn_A_log, dn_dt_bias, dn_norm_w,
        w_out, norm2_w, w_router, b_router, w_gate_up, b_gate_up, w_down, b_down, 0)
    out = _combine(dest_flat, h1, gates, ys, final_norm_w[None])
    return out.reshape(B, S, D)
```

```python
import functools

import jax
import jax.numpy as jnp
from jax import lax
from jax.experimental import pallas as pl
from jax.experimental.pallas import tpu as pltpu

F32 = jnp.float32
BF16 = jnp.bfloat16
I32 = jnp.int32

EPS = 1e-6
CHUNK = 64
HEAD_DIM = 128
N_HEADS = 16
CONV_W = 2048
DN_W = 2048
N_EXPERTS = 32
TOP_K = 4
EXPERT_FF = 2048
SWIGLU_LIMIT = 7.0
SWIGLU_ALPHA = 1.702

LANE = 128
VMEM_LIMIT = 56 * 1024 * 1024
SLOT_BLK = 128
MOE_SUB = SLOT_BLK
MOE_BLOCK_UNITS = 4
MOE_ITEM_SUBS = 10
MOE_DMA_SPLIT = 4
DMA_ISSUE_UNROLL = 4


def _cparams(sem):
    return pltpu.CompilerParams(dimension_semantics=sem, vmem_limit_bytes=VMEM_LIMIT)


def _mm(a, b):
    return jnp.dot(a.astype(BF16), b.astype(BF16), preferred_element_type=F32)


def _dot_nt(a, b):
    return lax.dot_general(a, b, (((1,), (1,)), ((), ())), preferred_element_type=F32)


def _mm_nt(a, b):
    return _dot_nt(a.astype(BF16), b.astype(BF16))


def _mm_tn(a, b):
    return lax.dot_general(a.astype(BF16), b.astype(BF16), (((0,), (0,)), ((), ())),
                           preferred_element_type=F32)


def _mm3_nt(x, wt):
    xh = x.astype(BF16)
    xl = (x - xh.astype(F32)).astype(BF16)
    wh = wt.astype(BF16)
    wl = (wt - wh.astype(F32)).astype(BF16)
    return _dot_nt(xh, wh) + _dot_nt(xl, wh) + _dot_nt(xh, wl)


def _sigmoid(x):
    return 1.0 / (1.0 + jnp.exp(-x))


def _norm1_kernel(x_ref, nw_ref, wab_ref, alog_ref, dtb_ref, xn_ref, gc_ref, beta_ref):
    x = x_ref[...]
    ms = jnp.mean(x * x, axis=-1, keepdims=True)
    xn = x * lax.rsqrt(ms + EPS) * nw_ref[...]
    xn_ref[...] = xn.astype(BF16)
    ab = _mm3_nt(xn, wab_ref[...])
    a = ab[:, :N_HEADS] + dtb_ref[...]
    b = ab[:, N_HEADS:]
    softplus = jnp.maximum(a, 0.0) + jnp.log1p(jnp.exp(-jnp.abs(a)))
    g = -jnp.exp(alog_ref[...]) * softplus
    pos = lax.broadcasted_iota(I32, g.shape, 0) % CHUNK
    shift = 1
    while shift < CHUNK:
        g = g + jnp.where(pos >= shift, pltpu.roll(g, shift, 0), 0.0)
        shift *= 2
    gc_ref[...] = g
    beta_ref[...] = _sigmoid(b)


def _norm1(x, nw, w_in_t, ab_row, alog, dtb, bm=256):
    T, D = x.shape
    nab = 2 * N_HEADS
    return pl.pallas_call(
        _norm1_kernel,
        grid=(T // bm,),
        in_specs=[pl.BlockSpec((bm, D), lambda i: (i, 0)),
                  pl.BlockSpec((1, D), lambda i: (0, 0)),
                  pl.BlockSpec((nab, D), lambda i: (ab_row // nab, 0)),
                  pl.BlockSpec((1, N_HEADS), lambda i: (0, 0)),
                  pl.BlockSpec((1, N_HEADS), lambda i: (0, 0))],
        out_specs=[pl.BlockSpec((bm, D), lambda i: (i, 0)),
                   pl.BlockSpec((bm, N_HEADS), lambda i: (i, 0)),
                   pl.BlockSpec((bm, N_HEADS), lambda i: (i, 0))],
        out_shape=[jax.ShapeDtypeStruct((T, D), BF16),
                   jax.ShapeDtypeStruct((T, N_HEADS), F32),
                   jax.ShapeDtypeStruct((T, N_HEADS), F32)],
        compiler_params=_cparams(("arbitrary",)),
        name="norm1",
    )(x, nw, w_in_t, alog, dtb)


def _in_proj_kernel(x_ref, w_ref, o_ref, wb_ref):
    @pl.when(pl.program_id(1) == 0)
    def _():
        wb_ref[...] = w_ref[...].astype(BF16)

    o_ref[...] = _dot_nt(x_ref[...], wb_ref[...])


def _in_proj(xn, w_in_t, n_cols, bm=1024, bn=512):
    T, D = xn.shape
    return pl.pallas_call(
        _in_proj_kernel,
        grid=(n_cols // bn, T // bm),
        in_specs=[pl.BlockSpec((bm, D), lambda j, i: (i, 0)),
                  pl.BlockSpec((bn, D), lambda j, i: (j, 0))],
        out_specs=pl.BlockSpec((bm, bn), lambda j, i: (i, j)),
        out_shape=jax.ShapeDtypeStruct((T, n_cols), F32),
        scratch_shapes=[pltpu.VMEM((bn, D), BF16)],
        compiler_params=_cparams(("arbitrary", "arbitrary")),
        name="in_proj",
    )(xn, w_in_t)


def _conv_mix_kernel(x_ref, b_ref, c_ref, xh_ref, ch_ref, cw_ref, nw_ref, o_ref):
    bm, bc = x_ref.shape
    u = c_ref[...] * x_ref[...]
    halo = jnp.where(pl.program_id(0) > 0, ch_ref[...] * xh_ref[...], 0.0)
    ext = jnp.concatenate([halo, u], axis=0)
    cw = cw_ref[...]
    conv = ext[6:6 + bm] * cw[0:1] + ext[7:7 + bm] * cw[1:2] + u * cw[2:3]
    y = b_ref[...] * conv
    nw = nw_ref[...]
    for g in range(bc // LANE):
        cs = slice(g * LANE, (g + 1) * LANE)
        yg = y[:, cs]
        ms = jnp.mean(yg * yg, axis=-1, keepdims=True)
        o_ref[:, cs] = (yg * lax.rsqrt(ms + EPS) * nw[:, cs]).astype(BF16)


def _conv_mix(proj, conv_w, norm_w, bm=512, bc=512):
    T = proj.shape[0]
    nc = CONV_W // bc
    hb = bm // 8
    halo_map = lambda off: (lambda i, j: (jnp.maximum(i * hb - 1, 0), j + off))
    return pl.pallas_call(
        _conv_mix_kernel,
        grid=(T // bm, nc),
        in_specs=[pl.BlockSpec((bm, bc), lambda i, j: (i, j)),
                  pl.BlockSpec((bm, bc), lambda i, j: (i, j + nc)),
                  pl.BlockSpec((bm, bc), lambda i, j: (i, j + 2 * nc)),
                  pl.BlockSpec((8, bc), halo_map(0)),
                  pl.BlockSpec((8, bc), halo_map(2 * nc)),
                  pl.BlockSpec((3, bc), lambda i, j: (0, j)),
                  pl.BlockSpec((1, bc), lambda i, j: (0, j))],
        out_specs=pl.BlockSpec((bm, bc), lambda i, j: (i, j)),
        out_shape=jax.ShapeDtypeStruct((T, CONV_W), BF16),
        compiler_params=_cparams(("arbitrary", "arbitrary")),
        name="conv_mix",
    )(proj, proj, proj, proj, proj, conv_w, norm_w)


def _gdn_kernel(q_ref, k_ref, v_ref, z_ref, gc_ref, beta_ref, gct_ref, cw_ref, nw_ref,
                y_ref, qs_ref, ks_ref, vs_ref, halo_ref, s_ref):
    tb = q_ref.shape[0]
    t = pl.program_id(0)

    @pl.when(t == 0)
    def _():
        halo_ref[...] = jnp.zeros(halo_ref.shape, F32)
        s_ref[...] = jnp.zeros(s_ref.shape, F32)

    for a, (src, dst) in enumerate(((q_ref, qs_ref), (k_ref, ks_ref), (v_ref, vs_ref))):
        for h in range(N_HEADS):
            cs = slice(h * HEAD_DIM, (h + 1) * HEAD_DIM)
            raw = src[:, cs]
            ext = jnp.concatenate([halo_ref[a, :, cs], raw], axis=0)
            cw = cw_ref[:, a * DN_W + h * HEAD_DIM:a * DN_W + (h + 1) * HEAD_DIM]
            c = (ext[5:5 + tb] * cw[0:1] + ext[6:6 + tb] * cw[1:2]
                 + ext[7:7 + tb] * cw[2:3] + raw * cw[3:4])
            c = c * _sigmoid(c)
            if a < 2:
                c = c * lax.rsqrt(jnp.sum(c * c, axis=-1, keepdims=True) + EPS)
            if a == 0:
                c = c * (HEAD_DIM ** -0.5)
            dst[:, cs] = c
            halo_ref[a, :, cs] = raw[tb - 8:tb]

    ii = lax.broadcasted_iota(I32, (CHUNK, CHUNK), 0)
    jj = lax.broadcasted_iota(I32, (CHUNK, CHUNK), 1)
    eye = (ii == jj).astype(F32)
    nw = nw_ref[...]

    heads = range(N_HEADS)
    cols = [slice(h * HEAD_DIM, (h + 1) * HEAD_DIM) for h in heads]

    def chunk(c, carry):
        rows = pl.ds(pl.multiple_of(c * CHUNK, CHUNK), CHUNK)
        cg = t * (tb // CHUNK) + c
        gcb = gc_ref[rows, :]
        btb = beta_ref[rows, :]
        q = [qs_ref[rows, cs] for cs in cols]
        k = [ks_ref[rows, cs] for cs in cols]
        v = [vs_ref[rows, cs] for cs in cols]
        z = [z_ref[rows, cs] for cs in cols]
        s = [s_ref[h] for h in heads]
        gcol = [gcb[:, h:h + 1] for h in heads]
        bcol = [btb[:, h:h + 1] for h in heads]
        grow = [gct_ref[h, pl.ds(cg, 1), :] for h in heads]
        gl = [g[:, CHUNK - 1:CHUNK] for g in grow]
        eg = [jnp.exp(g) for g in gcol]
        kb = [k[h] * bcol[h] for h in heads]
        a1 = [_mm_nt(jnp.concatenate([kb[h], q[h]], axis=0), k[h]) for h in heads]
        dec = [jnp.where(ii >= jj, jnp.exp(jnp.minimum(gcol[h] - grow[h], 0.0)), 0.0) for h in heads]
        lmat = [jnp.where(ii > jj, a1[h][:CHUNK] * dec[h], 0.0) for h in heads]
        qkd = [a1[h][CHUNK:] * dec[h] for h in heads]
        p = [eye - lmat[h] for h in heads]
        cur = [_mm(lmat[h], lmat[h]) for h in heads]
        power = 2
        while 2 * power < CHUNK:
            r = [_mm(jnp.concatenate([p[h], cur[h]], axis=0), cur[h]) for h in heads]
            p = [p[h] + r[h][:CHUNK] for h in heads]
            cur = [r[h][CHUNK:] for h in heads]
            power *= 2
        p = [p[h] + _mm(p[h], cur[h]) for h in heads]
        uw = [_mm(p[h], jnp.concatenate([v[h] * bcol[h], kb[h] * eg[h]], axis=1)) for h in heads]
        ws = [_mm(jnp.concatenate([uw[h][:, HEAD_DIM:], q[h] * eg[h]], axis=0), s[h]) for h in heads]
        vn = [uw[h][:, :HEAD_DIM] - ws[h][:CHUNK] for h in heads]
        o = [ws[h][CHUNK:] + _mm(qkd[h], vn[h]) for h in heads]
        kt = [k[h] * jnp.exp(gl[h] - gcol[h]) for h in heads]
        s_new = [s[h] * jnp.exp(gl[h]) + _mm_tn(kt[h], vn[h]) for h in heads]
        for h in heads:
            on = o[h] * lax.rsqrt(jnp.mean(o[h] * o[h], axis=-1, keepdims=True) + EPS) * nw
            y_ref[rows, cols[h]] = (on * (z[h] * _sigmoid(z[h]))).astype(BF16)
            s_ref[h] = s_new[h]
        return carry

    lax.fori_loop(0, tb // CHUNK, chunk, 0)


def _gdn(proj, gc, beta, gct, conv_w, norm_w, tb=256):
    T = proj.shape[0]
    qb = (3 * CONV_W) // DN_W
    n_chunks = T // CHUNK
    return pl.pallas_call(
        _gdn_kernel,
        grid=(T // tb,),
        in_specs=[pl.BlockSpec((tb, DN_W), lambda t: (t, qb)),
                  pl.BlockSpec((tb, DN_W), lambda t: (t, qb + 1)),
                  pl.BlockSpec((tb, DN_W), lambda t: (t, qb + 2)),
                  pl.BlockSpec((tb, DN_W), lambda t: (t, qb + 3)),
                  pl.BlockSpec((tb, N_HEADS), lambda t: (t, 0)),
                  pl.BlockSpec((tb, N_HEADS), lambda t: (t, 0)),
                  pl.BlockSpec((N_HEADS, n_chunks, CHUNK), lambda t: (0, 0, 0)),
                  pl.BlockSpec((4, 3 * DN_W), lambda t: (0, 0)),
                  pl.BlockSpec((1, HEAD_DIM), lambda t: (0, 0))],
        out_specs=pl.BlockSpec((tb, DN_W), lambda t: (t, 0)),
        out_shape=jax.ShapeDtypeStruct((T, DN_W), BF16),
        scratch_shapes=[pltpu.VMEM((tb, DN_W), F32),
                        pltpu.VMEM((tb, DN_W), F32),
                        pltpu.VMEM((tb, DN_W), F32),
                        pltpu.VMEM((3, 8, DN_W), F32),
                        pltpu.VMEM((N_HEADS, HEAD_DIM, HEAD_DIM), F32)],
        compiler_params=_cparams(("arbitrary",)),
        name="gdn",
    )(proj, proj, proj, proj, gc, beta, gct, conv_w, norm_w)


def _out_proj_kernel(ya_ref, yb_ref, w_ref, x_ref, o_ref, wb_ref):
    @pl.when(pl.program_id(1) == 0)
    def _():
        wb_ref[...] = w_ref[...].astype(BF16)

    o_ref[...] = (x_ref[...]
                  + jnp.dot(ya_ref[...], wb_ref[:CONV_W], preferred_element_type=F32)
                  + jnp.dot(yb_ref[...], wb_ref[CONV_W:], preferred_element_type=F32))


def _out_proj(ya, yb, w_out, x, bm=1024, bn=512):
    T, D = x.shape
    K = w_out.shape[0]
    return pl.pallas_call(
        _out_proj_kernel,
        grid=(D // bn, T // bm),
        in_specs=[pl.BlockSpec((bm, CONV_W), lambda j, i: (i, 0)),
                  pl.BlockSpec((bm, DN_W), lambda j, i: (i, 0)),
                  pl.BlockSpec((K, bn), lambda j, i: (0, j)),
                  pl.BlockSpec((bm, bn), lambda j, i: (i, j))],
        out_specs=pl.BlockSpec((bm, bn), lambda j, i: (i, j)),
        out_shape=jax.ShapeDtypeStruct((T, D), F32),
        scratch_shapes=[pltpu.VMEM((K, bn), BF16)],
        compiler_params=_cparams(("arbitrary", "arbitrary")),
        name="out_proj",
    )(ya, yb, w_out, x)


def _router_kernel(h_ref, nw_ref, wr_ref, br_ref, xp_ref, idx_ref, gate_ref, rank_ref, cnt_ref,
                   carry_ref):
    bm, D = h_ref.shape

    @pl.when(pl.program_id(0) == 0)
    def _():
        carry_ref[...] = jnp.zeros(carry_ref.shape, F32)

    h = h_ref[...]
    ms = jnp.mean(h * h, axis=-1, keepdims=True)
    xn = h * lax.rsqrt(ms + EPS) * nw_ref[...]
    xp_ref[...] = pltpu.pack_elementwise([xn[:, :D // 2], xn[:, D // 2:]], packed_dtype=BF16)

    logits = _mm3_nt(xn, wr_ref[...]) + br_ref[...]
    lane_e = lax.broadcasted_iota(I32, logits.shape, 1)
    work = logits
    sels, vals, idxs = [], [], []
    for _ in range(TOP_K):
        m = jnp.max(work, axis=-1, keepdims=True)
        idx = jnp.min(jnp.where(work == m, lane_e, N_EXPERTS), axis=-1, keepdims=True)
        sel = lane_e == idx
        work = jnp.where(sel, -jnp.inf, work)
        sels.append(sel)
        vals.append(m)
        idxs.append(idx)
    exps = [jnp.exp(v - vals[0]) for v in vals]
    denom = exps[0] + exps[1] + exps[2] + exps[3]
    gates = [e / denom for e in exps]

    onehot = jnp.zeros(logits.shape, F32)
    for sel in sels:
        onehot = onehot + sel.astype(F32)
    ri = lax.broadcasted_iota(I32, (bm, bm), 0)
    ci = lax.broadcasted_iota(I32, (bm, bm), 1)
    tri = (ri > ci).astype(BF16)
    carry = carry_ref[...]
    before = jnp.dot(tri, onehot.astype(BF16), preferred_element_type=F32) + carry
    ranks = [jnp.sum(jnp.where(sel, before, 0.0), axis=-1, keepdims=True).astype(I32)
             for sel in sels]
    new_carry = carry + jnp.sum(onehot, axis=0, keepdims=True)
    carry_ref[...] = new_carry
    cnt_ref[...] = new_carry

    lane = lax.broadcasted_iota(I32, (bm, LANE), 1)

    def spread(cols):
        return jnp.where(lane == 0, cols[0],
                         jnp.where(lane == 1, cols[1], jnp.where(lane == 2, cols[2], cols[3])))

    idx_ref[...] = spread(idxs)
    gate_ref[...] = spread(gates)
    rank_ref[...] = spread(ranks)


def _router(h1, nw, w_router_t, b_router, bm=256):
    T, D = h1.shape
    return pl.pallas_call(
        _router_kernel,
        grid=(T // bm,),
        in_specs=[pl.BlockSpec((bm, D), lambda i: (i, 0)),
                  pl.BlockSpec((1, D), lambda i: (0, 0)),
                  pl.BlockSpec((N_EXPERTS, D), lambda i: (0, 0)),
                  pl.BlockSpec((1, N_EXPERTS), lambda i: (0, 0))],
        out_specs=[pl.BlockSpec((bm, D // 2), lambda i: (i, 0)),
                   pl.BlockSpec((bm, LANE), lambda i: (i, 0)),
                   pl.BlockSpec((bm, LANE), lambda i: (i, 0)),
                   pl.BlockSpec((bm, LANE), lambda i: (i, 0)),
                   pl.BlockSpec((1, N_EXPERTS), lambda i: (0, 0))],
        out_shape=[jax.ShapeDtypeStruct((T, D // 2), jnp.uint32),
                   jax.ShapeDtypeStruct((T, LANE), I32),
                   jax.ShapeDtypeStruct((T, LANE), F32),
                   jax.ShapeDtypeStruct((T, LANE), I32),
                   jax.ShapeDtypeStruct((1, N_EXPERTS), F32)],
        scratch_shapes=[pltpu.VMEM((1, N_EXPERTS), F32)],
        compiler_params=_cparams(("arbitrary",)),
        name="router",
    )(h1, nw, w_router_t, b_router)


def _dispatch_kernel(dest_ref, pad_start_ref, pad_len_ref, nused_ref, x_ref, xs_ref, zero_ref, sem):
    tb = x_ref.shape[0]
    i = pl.program_id(0)
    n_blocks = xs_ref.shape[0] // SLOT_BLK

    def row_copy(r, slot):
        return pltpu.make_async_copy(x_ref.at[pl.ds(r, 1)], xs_ref.at[pl.ds(slot, 1)], sem)

    def zero_row_copy(slot):
        return pltpu.make_async_copy(zero_ref.at[pl.ds(0, 1)], xs_ref.at[pl.ds(slot, 1)], sem)

    def zero_blk_copy(b):
        start = pl.multiple_of(b * SLOT_BLK, SLOT_BLK)
        return pltpu.make_async_copy(zero_ref, xs_ref.at[pl.ds(start, SLOT_BLK)], sem)

    @pl.when(i == 0)
    def _():
        zero_ref[...] = jnp.zeros(zero_ref.shape, zero_ref.dtype)

        def per_expert(e, c):
            start = pad_start_ref[e]
            n = pad_len_ref[e]
            lax.fori_loop(0, n, lambda j, cc: (zero_row_copy(start + j).start(), cc)[1], 0)
            lax.fori_loop(0, n, lambda j, cc: (zero_row_copy(start + j).wait(), cc)[1], 0)
            return c

        lax.fori_loop(0, N_EXPERTS, per_expert, 0)
        nu = nused_ref[0]
        lax.fori_loop(nu, n_blocks, lambda b, cc: (zero_blk_copy(b).start(), cc)[1], 0)
        lax.fori_loop(nu, n_blocks, lambda b, cc: (zero_blk_copy(b).wait(), cc)[1], 0)

    def issue(r, c):
        base = (i * tb + r) * TOP_K
        for k in range(TOP_K):
            row_copy(r, dest_ref[base + k]).start()
        return c

    def drain(r, c):
        for k in range(TOP_K):
            row_copy(0, 0).wait()
        return c

    lax.fori_loop(0, tb, issue, 0, unroll=DMA_ISSUE_UNROLL)
    lax.fori_loop(0, tb, drain, 0, unroll=DMA_ISSUE_UNROLL)


def _dispatch(dest_flat, pad_start, pad_len, nused, xp, n_slots, tb=256):
    T, W = xp.shape
    return pl.pallas_call(
        _dispatch_kernel,
        grid_spec=pltpu.PrefetchScalarGridSpec(
            num_scalar_prefetch=4,
            grid=(T // tb,),
            in_specs=[pl.BlockSpec((tb, W), lambda i, *_: (i, 0))],
            out_specs=pl.BlockSpec(memory_space=pl.ANY),
            scratch_shapes=[pltpu.VMEM((SLOT_BLK, W), xp.dtype),
                            pltpu.SemaphoreType.DMA(())],
        ),
        out_shape=jax.ShapeDtypeStruct((n_slots, W), xp.dtype),
        compiler_params=_cparams(("arbitrary",)),
        name="dispatch",
    )(dest_flat, pad_start, pad_len, nused, xp)


def _unpack_pair(xw):
    xa = pltpu.unpack_elementwise(xw, index=0, packed_dtype=BF16, unpacked_dtype=F32)
    xb = pltpu.unpack_elementwise(xw, index=1, packed_dtype=BF16, unpacked_dtype=F32)
    return xa.astype(BF16), xb.astype(BF16)


def _moe_kernel(item_e_ref, item_start_ref, item_nsub_ref, nused_ref,
                xs_ref, wgu_ref, wd_ref, bgu_ref, bd_ref, ys_ref,
                x_ref, h_ref, g_ref, gu_ref, d_ref, stage_ref, pend_ref,
                sem_x, sem_gu, sem_d, sem_o):
    i = pl.program_id(0)
    tf = gu_ref.shape[-1]
    tn = d_ref.shape[-1]
    half = gu_ref.shape[1] // 2
    nf = EXPERT_FF // tf
    gu_tiles = [(part, f) for f in range(nf) for part in range(2)]
    nn = wd_ref.shape[-1] // tn
    n_blocks = ys_ref.shape[0] // MOE_SUB

    e = item_e_ref[i]
    start = pl.multiple_of(item_start_ref[i], MOE_SUB)
    nsub = item_nsub_ref[i]
    nxt_e = item_e_ref[i + 1]
    nxt_start = pl.multiple_of(item_start_ref[i + 1], MOE_SUB)
    nxt_nsub = item_nsub_ref[i + 1]

    gu_rows = gu_ref.shape[1] // MOE_DMA_SPLIT
    d_rows = d_ref.shape[1] // MOE_DMA_SPLIT

    def gu_copies(ee, tile, slot):
        part, f = tile
        return [pltpu.make_async_copy(
            wgu_ref.at[0, ee, pl.ds(s * gu_rows, gu_rows), pl.ds(part * EXPERT_FF + f * tf, tf)],
            gu_ref.at[slot, pl.ds(s * gu_rows, gu_rows)], sem_gu.at[slot])
            for s in range(MOE_DMA_SPLIT)]

    def d_copies(ee, n, slot):
        return [pltpu.make_async_copy(
            wd_ref.at[0, ee, pl.ds(s * d_rows, d_rows), pl.ds(n * tn, tn)],
            d_ref.at[slot, pl.ds(s * d_rows, d_rows)], sem_d.at[slot])
            for s in range(MOE_DMA_SPLIT)]

    def x_copy(row0, r):
        off = pl.multiple_of(r * MOE_SUB, MOE_SUB)
        return pltpu.make_async_copy(xs_ref.at[pl.ds(row0 + off, MOE_SUB)],
                                     x_ref.at[pl.ds(off, MOE_SUB)], sem_x)

    def out_copy(row0, nrows, n, slot):
        return pltpu.make_async_copy(stage_ref.at[slot, pl.ds(0, nrows)],
                                     ys_ref.at[pl.ds(row0, nrows), pl.ds(n * tn, tn)],
                                     sem_o.at[slot])

    def wait_out(slot):
        for units in (1, 2, MOE_BLOCK_UNITS):
            @pl.when(pend_ref[slot] == units)
            def _(units=units):
                out_copy(start, units * MOE_SUB, 0, slot).wait()

    def for_row_blocks(body):
        nfull = nsub // MOE_BLOCK_UNITS
        rem = nsub % MOE_BLOCK_UNITS
        full_rows = MOE_BLOCK_UNITS * MOE_SUB

        def full(p, c):
            body(p, pl.multiple_of(p * full_rows, full_rows), full_rows)
            return c

        lax.fori_loop(0, nfull, full, 0)
        has2 = (rem // 2) % 2

        @pl.when(has2 == 1)
        def _():
            body(nfull, pl.multiple_of(nfull * full_rows, full_rows), 2 * MOE_SUB)

        @pl.when(rem % 2 == 1)
        def _():
            body(nfull + has2,
                 pl.multiple_of(nfull * full_rows + has2 * 2 * MOE_SUB, MOE_SUB), MOE_SUB)

    def start_gu(ee, f, slot):
        for cp in gu_copies(ee, f, slot):
            cp.start()

    def start_d(ee, n, slot):
        for cp in d_copies(ee, n, slot):
            cp.start()

    def load_rows(row0, count):
        lax.fori_loop(0, count, lambda r, c: (x_copy(row0, r).start(), c)[1], 0)

    @pl.when(i == 0)
    def _():
        pend_ref[0] = 0
        pend_ref[1] = 0
        stage_ref[0] = jnp.zeros(stage_ref.shape[1:], F32)

        def zero_block(b, c):
            row0 = pl.multiple_of(b * MOE_SUB, MOE_SUB)
            for n in range(nn):
                out_copy(row0, MOE_SUB, n, 0).start()
            for n in range(nn):
                out_copy(row0, MOE_SUB, n, 0).wait()
            return c

        lax.fori_loop(nused_ref[0], n_blocks, zero_block, 0)

        @pl.when(nsub > 0)
        def _():
            load_rows(start, nsub)
            start_gu(e, gu_tiles[0], 0)

    @pl.when(nsub > 0)
    def _():
        lax.fori_loop(0, nsub, lambda r, c: (x_copy(start, r).wait(), c)[1], 0)

        for t, (part, f) in enumerate(gu_tiles):
            slot = t % 2
            for cp in gu_copies(e, (part, f), slot):
                cp.wait()
            if t + 1 < len(gu_tiles):
                start_gu(e, gu_tiles[t + 1], 1 - slot)
            else:
                start_d(e, 0, 0)
            col0 = part * EXPERT_FF + f * tf
            bias = bgu_ref[pl.ds(e, 1), col0:col0 + tf]

            def up_rows(blk, off, nrows, part=part, f=f, bias=bias, slot=slot):
                rows = pl.ds(off, nrows)
                xa, xb = _unpack_pair(x_ref[rows, :])
                acc = (jnp.dot(xa, gu_ref[slot, :half].astype(BF16), preferred_element_type=F32)
                       + jnp.dot(xb, gu_ref[slot, half:].astype(BF16), preferred_element_type=F32)
                       + bias)
                if part == 0:
                    g_ref[rows, :] = jnp.minimum(acc, SWIGLU_LIMIT)
                else:
                    gate = g_ref[rows, :]
                    up = jnp.clip(acc, -SWIGLU_LIMIT, SWIGLU_LIMIT)
                    glu = gate * _sigmoid(SWIGLU_ALPHA * gate)
                    h_ref[rows, f * tf:(f + 1) * tf] = ((up + 1.0) * glu).astype(BF16)

            for_row_blocks(up_rows)

        @pl.when(nxt_nsub > 0)
        def _():
            load_rows(nxt_start, nxt_nsub)

        for n in range(nn):
            slot = n % 2
            for cp in d_copies(e, n, slot):
                cp.wait()
            if n + 1 < nn:
                start_d(e, n + 1, 1 - slot)
            else:
                @pl.when(nxt_nsub > 0)
                def _():
                    start_gu(nxt_e, gu_tiles[0], 0)
            bias_d = bd_ref[pl.ds(e, 1), n * tn:(n + 1) * tn]

            def down_rows(blk, off, nrows, n=n, bias_d=bias_d, slot=slot):
                os = blk % 2
                wait_out(os)
                stage_ref[os, pl.ds(0, nrows)] = (
                    jnp.dot(h_ref[pl.ds(off, nrows), :], d_ref[slot].astype(BF16),
                            preferred_element_type=F32) + bias_d)
                out_copy(start + off, nrows, n, os).start()
                pend_ref[os] = nrows // MOE_SUB

            for_row_blocks(down_rows)

        for os in range(2):
            wait_out(os)
            pend_ref[os] = 0


def _moe(item_e, item_start, item_nsub, nused, xs, w_gate_up, b_gate_up, w_down, b_down,
         tf=512, tn=512):
    n_slots, W = xs.shape
    D = 2 * W
    n_items = item_e.shape[0] - 1
    rb = MOE_ITEM_SUBS * MOE_SUB
    any_spec = pl.BlockSpec(memory_space=pl.ANY)
    return pl.pallas_call(
        _moe_kernel,
        grid_spec=pltpu.PrefetchScalarGridSpec(
            num_scalar_prefetch=4,
            grid=(n_items,),
            in_specs=[any_spec, any_spec, any_spec,
                      pl.BlockSpec(b_gate_up.shape, lambda i, *_: (0, 0)),
                      pl.BlockSpec(b_down.shape, lambda i, *_: (0, 0))],
            out_specs=any_spec,
            scratch_shapes=[pltpu.VMEM((rb, W), xs.dtype),
                            pltpu.VMEM((rb, EXPERT_FF), BF16),
                            pltpu.VMEM((rb, tf), F32),
                            pltpu.VMEM((2, D, tf), F32),
                            pltpu.VMEM((2, EXPERT_FF, tn), F32),
                            pltpu.VMEM((2, MOE_BLOCK_UNITS * MOE_SUB, tn), F32),
                            pltpu.SMEM((2,), I32),
                            pltpu.SemaphoreType.DMA(()),
                            pltpu.SemaphoreType.DMA((2,)),
                            pltpu.SemaphoreType.DMA((2,)),
                            pltpu.SemaphoreType.DMA((2,))],
        ),
        out_shape=jax.ShapeDtypeStruct((n_slots, D), F32),
        compiler_params=_cparams(("arbitrary",)),
        name="moe",
    )(item_e, item_start, item_nsub, nused, xs, w_gate_up, w_down, b_gate_up, b_down)


def _combine_kernel(dest_ref, h_ref, gate_ref, ys_ref, nw_ref, o_ref, buf_ref, sem):
    tb = h_ref.shape[0]
    i = pl.program_id(0)
    n_steps = pl.num_programs(0)

    def row_copy(slot, buf, k, r):
        return pltpu.make_async_copy(ys_ref.at[pl.ds(slot, 1)], buf_ref.at[buf, k, pl.ds(r, 1)],
                                     sem.at[buf])

    def gather(step, buf):
        def issue(r, c):
            base = (step * tb + r) * TOP_K
            for k in range(TOP_K):
                row_copy(dest_ref[base + k], buf, k, r).start()
            return c

        lax.fori_loop(0, tb, issue, 0, unroll=DMA_ISSUE_UNROLL)

    @pl.when(i == 0)
    def _():
        gather(0, 0)

    cur = i % 2

    @pl.when(i + 1 < n_steps)
    def _():
        gather(i + 1, 1 - cur)

    def drain(r, c):
        for k in range(TOP_K):
            row_copy(0, cur, k, r).wait()
        return c

    lax.fori_loop(0, tb, drain, 0, unroll=DMA_ISSUE_UNROLL)

    gates = gate_ref[...]
    h = h_ref[...]
    for k in range(TOP_K):
        h = h + buf_ref[cur, k] * gates[:, k:k + 1]
    ms = jnp.mean(h * h, axis=-1, keepdims=True)
    o_ref[...] = h * lax.rsqrt(ms + EPS) * nw_ref[...]


def _combine(dest_flat, h1, gates, ys, nw, tb=128):
    T, D = h1.shape
    return pl.pallas_call(
        _combine_kernel,
        grid_spec=pltpu.PrefetchScalarGridSpec(
            num_scalar_prefetch=1,
            grid=(T // tb,),
            in_specs=[pl.BlockSpec((tb, D), lambda i, *_: (i, 0)),
                      pl.BlockSpec((tb, LANE), lambda i, *_: (i, 0)),
                      pl.BlockSpec(memory_space=pl.ANY),
                      pl.BlockSpec((1, D), lambda i, *_: (0, 0))],
            out_specs=pl.BlockSpec((tb, D), lambda i, *_: (i, 0)),
            scratch_shapes=[pltpu.VMEM((2, TOP_K, tb, D), F32),
                            pltpu.SemaphoreType.DMA((2,))],
        ),
        out_shape=jax.ShapeDtypeStruct((T, D), F32),
        compiler_params=_cparams(("arbitrary",)),
        name="combine",
    )(dest_flat, h1, gates, ys, nw)


def _layer(h, norm1_w, w_in, conv_a_w, conv_a_norm_w, dn_conv_w, dn_A_log, dn_dt_bias, dn_norm_w,
           w_out, norm2_w, w_router, b_router, w_gate_up, b_gate_up, w_down, b_down, l):
    T, D = h.shape
    main_cols = 3 * CONV_W + 4 * DN_W
    w_in_t = w_in[l].T
    xn, gc, beta = _norm1(h, norm1_w[l][None], w_in_t, main_cols, dn_A_log[l][None],
                          dn_dt_bias[l][None])
    proj = _in_proj(xn, w_in_t, main_cols)
    ya = _conv_mix(proj, conv_a_w[l], conv_a_norm_w[l][None])
    gct = gc.T.reshape(N_HEADS, T // CHUNK, CHUNK)
    yb = _gdn(proj, gc, beta, gct, dn_conv_w[l], dn_norm_w[l][None])
    h1 = _out_proj(ya, yb, w_out[l], h)

    xp, idx, gates, rank, counts = _router(h1, norm2_w[l][None], w_router[l].T, b_router[l][None])
    top_idx = idx[:, :TOP_K]
    counts = counts[0].astype(I32)
    padded = ((counts + SLOT_BLK - 1) // SLOT_BLK) * SLOT_BLK
    pend = jnp.cumsum(padded)
    pstart = pend - padded
    dest_flat = (pstart[top_idx] + rank[:, :TOP_K]).reshape(-1)
    n_slots = T * TOP_K + N_EXPERTS * SLOT_BLK
    nb = n_slots // SLOT_BLK
    nused = (pend[-1] // SLOT_BLK).reshape(1)

    nsub_e = padded // SLOT_BLK
    items_e = (nsub_e + MOE_ITEM_SUBS - 1) // MOE_ITEM_SUBS
    item_end = jnp.cumsum(items_e)
    item_first = item_end - items_e
    n_items = N_EXPERTS + nb // MOE_ITEM_SUBS
    item = jnp.arange(n_items + 1, dtype=I32)
    item_e = jnp.minimum(jnp.sum((item_end[None, :] <= item[:, None]).astype(I32), axis=1),
                         N_EXPERTS - 1)
    item_j = item - item_first[item_e]
    item_valid = item < item_end[-1]
    item_start = jnp.where(item_valid, pstart[item_e] + item_j * (MOE_ITEM_SUBS * SLOT_BLK), 0)
    item_nsub = jnp.where(item_valid,
                          jnp.minimum(nsub_e[item_e] - item_j * MOE_ITEM_SUBS, MOE_ITEM_SUBS), 0)

    xs = _dispatch(dest_flat, pstart + counts, padded - counts, nused, xp, n_slots)
    ys = _moe(item_e, item_start.astype(I32), item_nsub.astype(I32), nused, xs,
              w_gate_up[l:l + 1], b_gate_up[l], w_down[l:l + 1], b_down[l])
    return h1, dest_flat, gates, ys


def kernel(x, norm1_w, w_in, conv_a_w, conv_a_norm_w, dn_conv_w, dn_A_log, dn_dt_bias, dn_norm_w,
           w_out, norm2_w, w_router, b_router, w_gate_up, b_gate_up, w_down, b_down, final_norm_w):
    B, S, D = x.shape
    depth = norm1_w.shape[0]
    assert depth == 1, "the final norm is fused into the last layer's combine"
    h = x.reshape(B * S, D)
    h1, dest_flat, gates, ys = _layer(
        h, norm1_w, w_in, conv_a_w, conv_a_norm_w, dn_conv_w, dn_A_log, dn_dt_bias, dn_norm_w,
        w_out, norm2_w, w_router, b_router, w_gate_up, b_gate_up, w_down, b_down, 0)
    out = _combine(dest_flat, h1, gates, ys, final_norm_w[None])
    return out.reshape(B, S, D)
```

```python
import functools

import jax
import jax.numpy as jnp
from jax import lax
from jax.experimental import pallas as pl
from jax.experimental.pallas import tpu as pltpu

F32 = jnp.float32
BF16 = jnp.bfloat16
I32 = jnp.int32

EPS = 1e-6
CHUNK = 64
HEAD_DIM = 128
N_HEADS = 16
CONV_W = 2048
DN_W = 2048
N_EXPERTS = 32
TOP_K = 4
EXPERT_FF = 2048
SWIGLU_LIMIT = 7.0
SWIGLU_ALPHA = 1.702

LANE = 128
VMEM_LIMIT = 56 * 1024 * 1024
SLOT_BLK = 128
MOE_SUB = SLOT_BLK
MOE_BLOCK_UNITS = 4
MOE_ITEM_SUBS = 10
MOE_DMA_SPLIT = 4
DMA_ISSUE_UNROLL = 4


def _cparams(sem):
    return pltpu.CompilerParams(dimension_semantics=sem, vmem_limit_bytes=VMEM_LIMIT)


def _mm(a, b):
    return jnp.dot(a.astype(BF16), b.astype(BF16), preferred_element_type=F32)


def _dot_nt(a, b):
    return lax.dot_general(a, b, (((1,), (1,)), ((), ())), preferred_element_type=F32)


def _mm_nt(a, b):
    return _dot_nt(a.astype(BF16), b.astype(BF16))


def _mm_tn(a, b):
    return lax.dot_general(a.astype(BF16), b.astype(BF16), (((0,), (0,)), ((), ())),
                           preferred_element_type=F32)


def _mm3_nt(x, wt):
    xh = x.astype(BF16)
    xl = (x - xh.astype(F32)).astype(BF16)
    wh = wt.astype(BF16)
    wl = (wt - wh.astype(F32)).astype(BF16)
    return _dot_nt(xh, wh) + _dot_nt(xl, wh) + _dot_nt(xh, wl)


def _sigmoid(x):
    return 1.0 / (1.0 + jnp.exp(-x))


def _norm1_kernel(x_ref, nw_ref, wab_ref, alog_ref, dtb_ref, xn_ref, gc_ref, beta_ref):
    x = x_ref[...]
    ms = jnp.mean(x * x, axis=-1, keepdims=True)
    xn = x * lax.rsqrt(ms + EPS) * nw_ref[...]
    xn_ref[...] = xn.astype(BF16)
    ab = _mm3_nt(xn, wab_ref[...])
    a = ab[:, :N_HEADS] + dtb_ref[...]
    b = ab[:, N_HEADS:]
    softplus = jnp.maximum(a, 0.0) + jnp.log1p(jnp.exp(-jnp.abs(a)))
    g = -jnp.exp(alog_ref[...]) * softplus
    pos = lax.broadcasted_iota(I32, g.shape, 0) % CHUNK
    shift = 1
    while shift < CHUNK:
        g = g + jnp.where(pos >= shift, pltpu.roll(g, shift, 0), 0.0)
        shift *= 2
    gc_ref[...] = g
    beta_ref[...] = _sigmoid(b)


def _norm1(x, nw, w_in_t, ab_row, alog, dtb, bm=256):
    T, D = x.shape
    nab = 2 * N_HEADS
    return pl.pallas_call(
        _norm1_kernel,
        grid=(T // bm,),
        in_specs=[pl.BlockSpec((bm, D), lambda i: (i, 0)),
                  pl.BlockSpec((1, D), lambda i: (0, 0)),
                  pl.BlockSpec((nab, D), lambda i: (ab_row // nab, 0)),
                  pl.BlockSpec((1, N_HEADS), lambda i: (0, 0)),
                  pl.BlockSpec((1, N_HEADS), lambda i: (0, 0))],
        out_specs=[pl.BlockSpec((bm, D), lambda i: (i, 0)),
                   pl.BlockSpec((bm, N_HEADS), lambda i: (i, 0)),
                   pl.BlockSpec((bm, N_HEADS), lambda i: (i, 0))],
        out_shape=[jax.ShapeDtypeStruct((T, D), BF16),
                   jax.ShapeDtypeStruct((T, N_HEADS), F32),
                   jax.ShapeDtypeStruct((T, N_HEADS), F32)],
        compiler_params=_cparams(("arbitrary",)),
        name="norm1",
    )(x, nw, w_in_t, alog, dtb)


def _in_proj_kernel(x_ref, w_ref, o_ref, wb_ref):
    @pl.when(pl.program_id(1) == 0)
    def _():
        wb_ref[...] = w_ref[...].astype(BF16)

    o_ref[...] = _dot_nt(x_ref[...], wb_ref[...])


def _in_proj(xn, w_in_t, n_cols, bm=1024, bn=512):
    T, D = xn.shape
    return pl.pallas_call(
        _in_proj_kernel,
        grid=(n_cols // bn, T // bm),
        in_specs=[pl.BlockSpec((bm, D), lambda j, i: (i, 0)),
                  pl.BlockSpec((bn, D), lambda j, i: (j, 0))],
        out_specs=pl.BlockSpec((bm, bn), lambda j, i: (i, j)),
        out_shape=jax.ShapeDtypeStruct((T, n_cols), F32),
        scratch_shapes=[pltpu.VMEM((bn, D), BF16)],
        compiler_params=_cparams(("arbitrary", "arbitrary")),
        name="in_proj",
    )(xn, w_in_t)


def _conv_mix_kernel(x_ref, b_ref, c_ref, xh_ref, ch_ref, cw_ref, nw_ref, o_ref):
    bm, bc = x_ref.shape
    u = c_ref[...] * x_ref[...]
    halo = jnp.where(pl.program_id(0) > 0, ch_ref[...] * xh_ref[...], 0.0)
    ext = jnp.concatenate([halo, u], axis=0)
    cw = cw_ref[...]
    conv = ext[6:6 + bm] * cw[0:1] + ext[7:7 + bm] * cw[1:2] + u * cw[2:3]
    y = b_ref[...] * conv
    nw = nw_ref[...]
    for g in range(bc // LANE):
        cs = slice(g * LANE, (g + 1) * LANE)
        yg = y[:, cs]
        ms = jnp.mean(yg * yg, axis=-1, keepdims=True)
        o_ref[:, cs] = (yg * lax.rsqrt(ms + EPS) * nw[:, cs]).astype(BF16)


def _conv_mix(proj, conv_w, norm_w, bm=512, bc=512):
    T = proj.shape[0]
    nc = CONV_W // bc
    hb = bm // 8
    halo_map = lambda off: (lambda i, j: (jnp.maximum(i * hb - 1, 0), j + off))
    return pl.pallas_call(
        _conv_mix_kernel,
        grid=(T // bm, nc),
        in_specs=[pl.BlockSpec((bm, bc), lambda i, j: (i, j)),
                  pl.BlockSpec((bm, bc), lambda i, j: (i, j + nc)),
                  pl.BlockSpec((bm, bc), lambda i, j: (i, j + 2 * nc)),
                  pl.BlockSpec((8, bc), halo_map(0)),
                  pl.BlockSpec((8, bc), halo_map(2 * nc)),
                  pl.BlockSpec((3, bc), lambda i, j: (0, j)),
                  pl.BlockSpec((1, bc), lambda i, j: (0, j))],
        out_specs=pl.BlockSpec((bm, bc), lambda i, j: (i, j)),
        out_shape=jax.ShapeDtypeStruct((T, CONV_W), BF16),
        compiler_params=_cparams(("arbitrary", "arbitrary")),
        name="conv_mix",
    )(proj, proj, proj, proj, proj, conv_w, norm_w)


def _gdn_kernel(q_ref, k_ref, v_ref, z_ref, gc_ref, beta_ref, gct_ref, cw_ref, nw_ref,
                y_ref, qs_ref, ks_ref, vs_ref, halo_ref, s_ref):
    tb = q_ref.shape[0]
    t = pl.program_id(0)

    @pl.when(t == 0)
    def _():
        halo_ref[...] = jnp.zeros(halo_ref.shape, F32)
        s_ref[...] = jnp.zeros(s_ref.shape, F32)

    for a, (src, dst) in enumerate(((q_ref, qs_ref), (k_ref, ks_ref), (v_ref, vs_ref))):
        for h in range(N_HEADS):
            cs = slice(h * HEAD_DIM, (h + 1) * HEAD_DIM)
            raw = src[:, cs]
            ext = jnp.concatenate([halo_ref[a, :, cs], raw], axis=0)
            cw = cw_ref[:, a * DN_W + h * HEAD_DIM:a * DN_W + (h + 1) * HEAD_DIM]
            c = (ext[5:5 + tb] * cw[0:1] + ext[6:6 + tb] * cw[1:2]
                 + ext[7:7 + tb] * cw[2:3] + raw * cw[3:4])
            c = c * _sigmoid(c)
            if a < 2:
                c = c * lax.rsqrt(jnp.sum(c * c, axis=-1, keepdims=True) + EPS)
            if a == 0:
                c = c * (HEAD_DIM ** -0.5)
            dst[:, cs] = c
            halo_ref[a, :, cs] = raw[tb - 8:tb]

    ii = lax.broadcasted_iota(I32, (CHUNK, CHUNK), 0)
    jj = lax.broadcasted_iota(I32, (CHUNK, CHUNK), 1)
    eye = (ii == jj).astype(F32)
    nw = nw_ref[...]

    heads = range(N_HEADS)
    cols = [slice(h * HEAD_DIM, (h + 1) * HEAD_DIM) for h in heads]

    def chunk(c, carry):
        rows = pl.ds(pl.multiple_of(c * CHUNK, CHUNK), CHUNK)
        cg = t * (tb // CHUNK) + c
        gcb = gc_ref[rows, :]
        btb = beta_ref[rows, :]
        q = [qs_ref[rows, cs] for cs in cols]
        k = [ks_ref[rows, cs] for cs in cols]
        v = [vs_ref[rows, cs] for cs in cols]
        z = [z_ref[rows, cs] for cs in cols]
        s = [s_ref[h] for h in heads]
        gcol = [gcb[:, h:h + 1] for h in heads]
        bcol = [btb[:, h:h + 1] for h in heads]
        grow = [gct_ref[h, pl.ds(cg, 1), :] for h in heads]
        gl = [g[:, CHUNK - 1:CHUNK] for g in grow]
        eg = [jnp.exp(g) for g in gcol]
        kb = [k[h] * bcol[h] for h in heads]
        a1 = [_mm_nt(jnp.concatenate([kb[h], q[h]], axis=0), k[h]) for h in heads]
        dec = [jnp.where(ii >= jj, jnp.exp(jnp.minimum(gcol[h] - grow[h], 0.0)), 0.0) for h in heads]
        lmat = [jnp.where(ii > jj, a1[h][:CHUNK] * dec[h], 0.0) for h in heads]
        qkd = [a1[h][CHUNK:] * dec[h] for h in heads]
        p = [eye - lmat[h] for h in heads]
        cur = [_mm(lmat[h], lmat[h]) for h in heads]
        power = 2
        while 2 * power < CHUNK:
            r = [_mm(jnp.concatenate([p[h], cur[h]], axis=0), cur[h]) for h in heads]
            p = [p[h] + r[h][:CHUNK] for h in heads]
            cur = [r[h][CHUNK:] for h in heads]
            power *= 2
        p = [p[h] + _mm(p[h], cur[h]) for h in heads]
        uw = [_mm(p[h], jnp.concatenate([v[h] * bcol[h], kb[h] * eg[h]], axis=1)) for h in heads]
        ws = [_mm(jnp.concatenate([uw[h][:, HEAD_DIM:], q[h] * eg[h]], axis=0), s[h]) for h in heads]
        vn = [uw[h][:, :HEAD_DIM] - ws[h][:CHUNK] for h in heads]
        o = [ws[h][CHUNK:] + _mm(qkd[h], vn[h]) for h in heads]
        kt = [k[h] * jnp.exp(gl[h] - gcol[h]) for h in heads]
        s_new = [s[h] * jnp.exp(gl[h]) + _mm_tn(kt[h], vn[h]) for h in heads]
        for h in heads:
            on = o[h] * lax.rsqrt(jnp.mean(o[h] * o[h], axis=-1, keepdims=True) + EPS) * nw
            y_ref[rows, cols[h]] = (on * (z[h] * _sigmoid(z[h]))).astype(BF16)
            s_ref[h] = s_new[h]
        return carry

    lax.fori_loop(0, tb // CHUNK, chunk, 0)


def _gdn(proj, gc, beta, gct, conv_w, norm_w, tb=256):
    T = proj.shape[0]
    qb = (3 * CONV_W) // DN_W
    n_chunks = T // CHUNK
    return pl.pallas_call(
        _gdn_kernel,
        grid=(T // tb,),
        in_specs=[pl.BlockSpec((tb, DN_W), lambda t: (t, qb)),
                  pl.BlockSpec((tb, DN_W), lambda t: (t, qb + 1)),
                  pl.BlockSpec((tb, DN_W), lambda t: (t, qb + 2)),
                  pl.BlockSpec((tb, DN_W), lambda t: (t, qb + 3)),
                  pl.BlockSpec((tb, N_HEADS), lambda t: (t, 0)),
                  pl.BlockSpec((tb, N_HEADS), lambda t: (t, 0)),
                  pl.BlockSpec((N_HEADS, n_chunks, CHUNK), lambda t: (0, 0, 0)),
                  pl.BlockSpec((4, 3 * DN_W), lambda t: (0, 0)),
                  pl.BlockSpec((1, HEAD_DIM), lambda t: (0, 0))],
        out_specs=pl.BlockSpec((tb, DN_W), lambda t: (t, 0)),
        out_shape=jax.ShapeDtypeStruct((T, DN_W), BF16),
        scratch_shapes=[pltpu.VMEM((tb, DN_W), F32),
                        pltpu.VMEM((tb, DN_W), F32),
                        pltpu.VMEM((tb, DN_W), F32),
                        pltpu.VMEM((3, 8, DN_W), F32),
                        pltpu.VMEM((N_HEADS, HEAD_DIM, HEAD_DIM), F32)],
        compiler_params=_cparams(("arbitrary",)),
        name="gdn",
    )(proj, proj, proj, proj, gc, beta, gct, conv_w, norm_w)


def _out_proj_kernel(ya_ref, yb_ref, w_ref, x_ref, o_ref, wb_ref):
    @pl.when(pl.program_id(1) == 0)
    def _():
        wb_ref[...] = w_ref[...].astype(BF16)

    o_ref[...] = (x_ref[...]
                  + jnp.dot(ya_ref[...], wb_ref[:CONV_W], preferred_element_type=F32)
                  + jnp.dot(yb_ref[...], wb_ref[CONV_W:], preferred_element_type=F32))


def _out_proj(ya, yb, w_out, x, bm=1024, bn=512):
    T, D = x.shape
    K = w_out.shape[0]
    return pl.pallas_call(
        _out_proj_kernel,
        grid=(D // bn, T // bm),
        in_specs=[pl.BlockSpec((bm, CONV_W), lambda j, i: (i, 0)),
                  pl.BlockSpec((bm, DN_W), lambda j, i: (i, 0)),
                  pl.BlockSpec((K, bn), lambda j, i: (0, j)),
                  pl.BlockSpec((bm, bn), lambda j, i: (i, j))],
        out_specs=pl.BlockSpec((bm, bn), lambda j, i: (i, j)),
        out_shape=jax.ShapeDtypeStruct((T, D), F32),
        scratch_shapes=[pltpu.VMEM((K, bn), BF16)],
        compiler_params=_cparams(("arbitrary", "arbitrary")),
        name="out_proj",
    )(ya, yb, w_out, x)


def _router_kernel(h_ref, nw_ref, wr_ref, br_ref, xp_ref, idx_ref, gate_ref, rank_ref, cnt_ref,
                   carry_ref):
    bm, D = h_ref.shape

    @pl.when(pl.program_id(0) == 0)
    def _():
        carry_ref[...] = jnp.zeros(carry_ref.shape, F32)

    h = h_ref[...]
    ms = jnp.mean(h * h, axis=-1, keepdims=True)
    xn = h * lax.rsqrt(ms + EPS) * nw_ref[...]
    xp_ref[...] = pltpu.pack_elementwise([xn[:, :D // 2], xn[:, D // 2:]], packed_dtype=BF16)

    logits = _mm3_nt(xn, wr_ref[...]) + br_ref[...]
    lane_e = lax.broadcasted_iota(I32, logits.shape, 1)
    work = logits
    sels, vals, idxs = [], [], []
    for _ in range(TOP_K):
        m = jnp.max(work, axis=-1, keepdims=True)
        idx = jnp.min(jnp.where(work == m, lane_e, N_EXPERTS), axis=-1, keepdims=True)
        sel = lane_e == idx
        work = jnp.where(sel, -jnp.inf, work)
        sels.append(sel)
        vals.append(m)
        idxs.append(idx)
    exps = [jnp.exp(v - vals[0]) for v in vals]
    denom = exps[0] + exps[1] + exps[2] + exps[3]
    gates = [e / denom for e in exps]

    onehot = jnp.zeros(logits.shape, F32)
    for sel in sels:
        onehot = onehot + sel.astype(F32)
    ri = lax.broadcasted_iota(I32, (bm, bm), 0)
    ci = lax.broadcasted_iota(I32, (bm, bm), 1)
    tri = (ri > ci).astype(BF16)
    carry = carry_ref[...]
    before = jnp.dot(tri, onehot.astype(BF16), preferred_element_type=F32) + carry
    ranks = [jnp.sum(jnp.where(sel, before, 0.0), axis=-1, keepdims=True).astype(I32)
             for sel in sels]
    new_carry = carry + jnp.sum(onehot, axis=0, keepdims=True)
    carry_ref[...] = new_carry
    cnt_ref[...] = new_carry

    lane = lax.broadcasted_iota(I32, (bm, LANE), 1)

    def spread(cols):
        return jnp.where(lane == 0, cols[0],
                         jnp.where(lane == 1, cols[1], jnp.where(lane == 2, cols[2], cols[3])))

    idx_ref[...] = spread(idxs)
    gate_ref[...] = spread(gates)
    rank_ref[...] = spread(ranks)


def _router(h1, nw, w_router_t, b_router, bm=256):
    T, D = h1.shape
    return pl.pallas_call(
        _router_kernel,
        grid=(T // bm,),
        in_specs=[pl.BlockSpec((bm, D), lambda i: (i, 0)),
                  pl.BlockSpec((1, D), lambda i: (0, 0)),
                  pl.BlockSpec((N_EXPERTS, D), lambda i: (0, 0)),
                  pl.BlockSpec((1, N_EXPERTS), lambda i: (0, 0))],
        out_specs=[pl.BlockSpec((bm, D // 2), lambda i: (i, 0)),
                   pl.BlockSpec((bm, LANE), lambda i: (i, 0)),
                   pl.BlockSpec((bm, LANE), lambda i: (i, 0)),
                   pl.BlockSpec((bm, LANE), lambda i: (i, 0)),
                   pl.BlockSpec((1, N_EXPERTS), lambda i: (0, 0))],
        out_shape=[jax.ShapeDtypeStruct((T, D // 2), jnp.uint32),
                   jax.ShapeDtypeStruct((T, LANE), I32),
                   jax.ShapeDtypeStruct((T, LANE), F32),
                   jax.ShapeDtypeStruct((T, LANE), I32),
                   jax.ShapeDtypeStruct((1, N_EXPERTS), F32)],
        scratch_shapes=[pltpu.VMEM((1, N_EXPERTS), F32)],
        compiler_params=_cparams(("arbitrary",)),
        name="router",
    )(h1, nw, w_router_t, b_router)


def _dispatch_kernel(dest_ref, pad_start_ref, pad_len_ref, nused_ref, x_ref, xs_ref, zero_ref, sem):
    tb = x_ref.shape[0]
    i = pl.program_id(0)
    n_blocks = xs_ref.shape[0] // SLOT_BLK

    def row_copy(r, slot):
        return pltpu.make_async_copy(x_ref.at[pl.ds(r, 1)], xs_ref.at[pl.ds(slot, 1)], sem)

    def zero_row_copy(slot):
        return pltpu.make_async_copy(zero_ref.at[pl.ds(0, 1)], xs_ref.at[pl.ds(slot, 1)], sem)

    def zero_blk_copy(b):
        start = pl.multiple_of(b * SLOT_BLK, SLOT_BLK)
        return pltpu.make_async_copy(zero_ref, xs_ref.at[pl.ds(start, SLOT_BLK)], sem)

    @pl.when(i == 0)
    def _():
        zero_ref[...] = jnp.zeros(zero_ref.shape, zero_ref.dtype)

        def per_expert(e, c):
            start = pad_start_ref[e]
            n = pad_len_ref[e]
            lax.fori_loop(0, n, lambda j, cc: (zero_row_copy(start + j).start(), cc)[1], 0)
            lax.fori_loop(0, n, lambda j, cc: (zero_row_copy(start + j).wait(), cc)[1], 0)
            return c

        lax.fori_loop(0, N_EXPERTS, per_expert, 0)
        nu = nused_ref[0]
        lax.fori_loop(nu, n_blocks, lambda b, cc: (zero_blk_copy(b).start(), cc)[1], 0)
        lax.fori_loop(nu, n_blocks, lambda b, cc: (zero_blk_copy(b).wait(), cc)[1], 0)

    def issue(r, c):
        base = (i * tb + r) * TOP_K
        for k in range(TOP_K):
            row_copy(r, dest_ref[base + k]).start()
        return c

    def drain(r, c):
        for k in range(TOP_K):
            row_copy(0, 0).wait()
        return c

    lax.fori_loop(0, tb, issue, 0, unroll=DMA_ISSUE_UNROLL)
    lax.fori_loop(0, tb, drain, 0, unroll=DMA_ISSUE_UNROLL)


def _dispatch(dest_flat, pad_start, pad_len, nused, xp, n_slots, tb=256):
    T, W = xp.shape
    return pl.pallas_call(
        _dispatch_kernel,
        grid_spec=pltpu.PrefetchScalarGridSpec(
            num_scalar_prefetch=4,
            grid=(T // tb,),
            in_specs=[pl.BlockSpec((tb, W), lambda i, *_: (i, 0))],
            out_specs=pl.BlockSpec(memory_space=pl.ANY),
            scratch_shapes=[pltpu.VMEM((SLOT_BLK, W), xp.dtype),
                            pltpu.SemaphoreType.DMA(())],
        ),
        out_shape=jax.ShapeDtypeStruct((n_slots, W), xp.dtype),
        compiler_params=_cparams(("arbitrary",)),
        name="dispatch",
    )(dest_flat, pad_start, pad_len, nused, xp)


def _unpack_pair(xw):
    xa = pltpu.unpack_elementwise(xw, index=0, packed_dtype=BF16, unpacked_dtype=F32)
    xb = pltpu.unpack_elementwise(xw, index=1, packed_dtype=BF16, unpacked_dtype=F32)
    return xa.astype(BF16), xb.astype(BF16)


def _moe_kernel(item_e_ref, item_start_ref, item_nsub_ref, nused_ref,
                xs_ref, wgu_ref, wd_ref, bgu_ref, bd_ref, ys_ref,
                x_ref, h_ref, g_ref, gu_ref, d_ref, stage_ref, pend_ref,
                sem_x, sem_gu, sem_d, sem_o):
    i = pl.program_id(0)
    tf = gu_ref.shape[-1]
    tn = d_ref.shape[-1]
    half = gu_ref.shape[1] // 2
    nf = EXPERT_FF // tf
    nn = wd_ref.shape[-1] // tn
    n_blocks = ys_ref.shape[0] // MOE_SUB
    GATE, UP = 0, 1

    e = item_e_ref[i]
    start = pl.multiple_of(item_start_ref[i], MOE_SUB)
    nsub = item_nsub_ref[i]
    nxt_e = item_e_ref[i + 1]
    nxt_start = pl.multiple_of(item_start_ref[i + 1], MOE_SUB)
    nxt_nsub = item_nsub_ref[i + 1]

    gu_rows = gu_ref.shape[1] // MOE_DMA_SPLIT
    d_rows = d_ref.shape[1] // MOE_DMA_SPLIT

    def gu_copies(ee, part, f, slot=None):
        slot = part if slot is None else slot
        col0 = part * EXPERT_FF + pl.multiple_of(f * tf, tf)
        return [pltpu.make_async_copy(
            wgu_ref.at[0, ee, pl.ds(s * gu_rows, gu_rows), pl.ds(col0, tf)],
            gu_ref.at[slot, pl.ds(s * gu_rows, gu_rows)], sem_gu.at[slot])
            for s in range(MOE_DMA_SPLIT)]

    def d_copies(ee, n, slot):
        return [pltpu.make_async_copy(
            wd_ref.at[0, ee, pl.ds(s * d_rows, d_rows), pl.ds(pl.multiple_of(n * tn, tn), tn)],
            d_ref.at[slot, pl.ds(s * d_rows, d_rows)], sem_d.at[slot])
            for s in range(MOE_DMA_SPLIT)]

    def x_copy(row0, r):
        off = pl.multiple_of(r * MOE_SUB, MOE_SUB)
        return pltpu.make_async_copy(xs_ref.at[pl.ds(row0 + off, MOE_SUB)],
                                     x_ref.at[pl.ds(off, MOE_SUB)], sem_x)

    def out_copy(row0, nrows, n, slot):
        return pltpu.make_async_copy(
            stage_ref.at[slot, pl.ds(0, nrows)],
            ys_ref.at[pl.ds(row0, nrows), pl.ds(pl.multiple_of(n * tn, tn), tn)], sem_o.at[slot])

    def wait_out(slot):
        for units in (1, 2, MOE_BLOCK_UNITS):
            @pl.when(pend_ref[slot] == units)
            def _(units=units):
                out_copy(start, units * MOE_SUB, 0, slot).wait()

    def for_row_blocks(body):
        nfull = nsub // MOE_BLOCK_UNITS
        rem = nsub % MOE_BLOCK_UNITS
        full_rows = MOE_BLOCK_UNITS * MOE_SUB

        def full(p, c):
            body(p, pl.multiple_of(p * full_rows, full_rows), full_rows)
            return c

        lax.fori_loop(0, nfull, full, 0)
        has2 = (rem // 2) % 2

        @pl.when(has2 == 1)
        def _():
            body(nfull, pl.multiple_of(nfull * full_rows, full_rows), 2 * MOE_SUB)

        @pl.when(rem % 2 == 1)
        def _():
            body(nfull + has2,
                 pl.multiple_of(nfull * full_rows + has2 * 2 * MOE_SUB, MOE_SUB), MOE_SUB)

    def start_gu(ee, part, f):
        for cp in gu_copies(ee, part, f):
            cp.start()

    def wait_gu(part):
        for cp in gu_copies(e, part, 0):
            cp.wait()

    def start_d(ee, n, slot):
        for cp in d_copies(ee, n, slot):
            cp.start()

    def load_rows(row0, count):
        lax.fori_loop(0, count, lambda r, c: (x_copy(row0, r).start(), c)[1], 0)

    @pl.when(i == 0)
    def _():
        pend_ref[0] = 0
        pend_ref[1] = 0
        stage_ref[0] = jnp.zeros(stage_ref.shape[1:], F32)

        def zero_block(b, c):
            row0 = pl.multiple_of(b * MOE_SUB, MOE_SUB)
            for n in range(nn):
                out_copy(row0, MOE_SUB, n, 0).start()
            for n in range(nn):
                out_copy(row0, MOE_SUB, n, 0).wait()
            return c

        lax.fori_loop(nused_ref[0], n_blocks, zero_block, 0)

        @pl.when(nsub > 0)
        def _():
            load_rows(start, nsub)
            start_gu(e, GATE, 0)

    def x_dot(rows, slot):
        xa, xb = _unpack_pair(x_ref[rows, :])
        return (jnp.dot(xa, gu_ref[slot, :half].astype(BF16), preferred_element_type=F32)
                + jnp.dot(xb, gu_ref[slot, half:].astype(BF16), preferred_element_type=F32))

    def gate_up_tiles(f, c):
        wait_gu(GATE)
        start_gu(e, UP, f)
        bias_g = bgu_ref[e, pl.ds(f, 1), :]

        def gate_rows(blk, off, nrows):
            rows = pl.ds(off, nrows)
            g_ref[rows, :] = jnp.minimum(x_dot(rows, GATE) + bias_g, SWIGLU_LIMIT)

        for_row_blocks(gate_rows)

        wait_gu(UP)

        @pl.when(f + 1 < nf)
        def _():
            start_gu(e, GATE, f + 1)

        @pl.when(f + 1 == nf)
        def _():
            start_d(e, 0, 0)

        bias_u = bgu_ref[e, pl.ds(nf + f, 1), :]

        def up_rows(blk, off, nrows):
            rows = pl.ds(off, nrows)
            gate = g_ref[rows, :]
            up = jnp.clip(x_dot(rows, UP) + bias_u, -SWIGLU_LIMIT, SWIGLU_LIMIT)
            glu = gate * _sigmoid(SWIGLU_ALPHA * gate)
            h_ref[f, rows, :] = ((up + 1.0) * glu).astype(BF16)

        for_row_blocks(up_rows)
        return c

    def down_tiles(n, c):
        slot = n % 2
        for cp in d_copies(e, n, slot):
            cp.wait()

        @pl.when(n + 1 < nn)
        def _():
            start_d(e, n + 1, 1 - slot)

        @pl.when(jnp.logical_and(n + 1 == nn, nxt_nsub > 0))
        def _():
            start_gu(nxt_e, GATE, 0)

        @pl.when(jnp.logical_and(n == 0, nxt_nsub > 0))
        def _():
            load_rows(nxt_start, nxt_nsub)

        bias_d = bd_ref[e, pl.ds(n, 1), :]

        def down_rows(blk, off, nrows):
            os = blk % 2
            wait_out(os)
            rows = pl.ds(off, nrows)
            acc = bias_d
            for f in range(nf):
                acc = acc + jnp.dot(h_ref[f, rows, :], d_ref[slot, f * tf:(f + 1) * tf, :].astype(BF16),
                                    preferred_element_type=F32)
            stage_ref[os, pl.ds(0, nrows)] = acc
            out_copy(start + off, nrows, n, os).start()
            pend_ref[os] = nrows // MOE_SUB

        for_row_blocks(down_rows)
        return c

    @pl.when(nsub > 0)
    def _():
        lax.fori_loop(0, nsub, lambda r, c: (x_copy(start, r).wait(), c)[1], 0)
        lax.fori_loop(0, nf, gate_up_tiles, 0)
        lax.fori_loop(0, nn, down_tiles, 0)
        for os in range(2):
            wait_out(os)
            pend_ref[os] = 0


def _moe(item_e, item_start, item_nsub, nused, xs, w_gate_up, b_gate_up, w_down, b_down,
         tf=512, tn=512):
    n_slots, W = xs.shape
    D = 2 * W
    n_items = item_e.shape[0] - 1
    rb = MOE_ITEM_SUBS * MOE_SUB
    b_gate_up = b_gate_up.reshape(N_EXPERTS, 2 * EXPERT_FF // tf, tf)
    b_down = b_down.reshape(N_EXPERTS, D // tn, tn)
    any_spec = pl.BlockSpec(memory_space=pl.ANY)
    return pl.pallas_call(
        _moe_kernel,
        grid_spec=pltpu.PrefetchScalarGridSpec(
            num_scalar_prefetch=4,
            grid=(n_items,),
            in_specs=[any_spec, any_spec, any_spec,
                      pl.BlockSpec(b_gate_up.shape, lambda i, *_: (0, 0, 0)),
                      pl.BlockSpec(b_down.shape, lambda i, *_: (0, 0, 0))],
            out_specs=any_spec,
            scratch_shapes=[pltpu.VMEM((rb, W), xs.dtype),
                            pltpu.VMEM((EXPERT_FF // tf, rb, tf), BF16),
                            pltpu.VMEM((rb, tf), F32),
                            pltpu.VMEM((2, D, tf), F32),
                            pltpu.VMEM((2, EXPERT_FF, tn), F32),
                            pltpu.VMEM((2, MOE_BLOCK_UNITS * MOE_SUB, tn), F32),
                            pltpu.SMEM((2,), I32),
                            pltpu.SemaphoreType.DMA(()),
                            pltpu.SemaphoreType.DMA((2,)),
                            pltpu.SemaphoreType.DMA((2,)),
                            pltpu.SemaphoreType.DMA((2,))],
        ),
        out_shape=jax.ShapeDtypeStruct((n_slots, D), F32),
        compiler_params=_cparams(("arbitrary",)),
        name="moe",
    )(item_e, item_start, item_nsub, nused, xs, w_gate_up, w_down, b_gate_up, b_down)


def _combine_kernel(dest_ref, h_ref, gate_ref, ys_ref, nw_ref, o_ref, buf_ref, sem):
    tb = h_ref.shape[0]
    i = pl.program_id(0)
    n_steps = pl.num_programs(0)

    def row_copy(slot, buf, k, r):
        return pltpu.make_async_copy(ys_ref.at[pl.ds(slot, 1)], buf_ref.at[buf, k, pl.ds(r, 1)],
                                     sem.at[buf])

    def gather(step, buf):
        def issue(r, c):
            base = (step * tb + r) * TOP_K
            for k in range(TOP_K):
                row_copy(dest_ref[base + k], buf, k, r).start()
            return c

        lax.fori_loop(0, tb, issue, 0, unroll=DMA_ISSUE_UNROLL)

    @pl.when(i == 0)
    def _():
        gather(0, 0)

    cur = i % 2

    @pl.when(i + 1 < n_steps)
    def _():
        gather(i + 1, 1 - cur)

    def drain(r, c):
        for k in range(TOP_K):
            row_copy(0, cur, k, r).wait()
        return c

    lax.fori_loop(0, tb, drain, 0, unroll=DMA_ISSUE_UNROLL)

    gates = gate_ref[...]
    h = h_ref[...]
    for k in range(TOP_K):
        h = h + buf_ref[cur, k] * gates[:, k:k + 1]
    ms = jnp.mean(h * h, axis=-1, keepdims=True)
    o_ref[...] = h * lax.rsqrt(ms + EPS) * nw_ref[...]


def _combine(dest_flat, h1, gates, ys, nw, tb=128):
    T, D = h1.shape
    return pl.pallas_call(
        _combine_kernel,
        grid_spec=pltpu.PrefetchScalarGridSpec(
            num_scalar_prefetch=1,
            grid=(T // tb,),
            in_specs=[pl.BlockSpec((tb, D), lambda i, *_: (i, 0)),
                      pl.BlockSpec((tb, LANE), lambda i, *_: (i, 0)),
                      pl.BlockSpec(memory_space=pl.ANY),
                      pl.BlockSpec((1, D), lambda i, *_: (0, 0))],
            out_specs=pl.BlockSpec((tb, D), lambda i, *_: (i, 0)),
            scratch_shapes=[pltpu.VMEM((2, TOP_K, tb, D), F32),
                            pltpu.SemaphoreType.DMA((2,))],
        ),
        out_shape=jax.ShapeDtypeStruct((T, D), F32),
        compiler_params=_cparams(("arbitrary",)),
        name="combine",
    )(dest_flat, h1, gates, ys, nw)


def _layer(h, norm1_w, w_in, conv_a_w, conv_a_norm_w, dn_conv_w, dn_A_log, dn_dt_bias, dn_norm_w,
           w_out, norm2_w, w_router, b_router, w_gate_up, b_gate_up, w_down, b_down, l):
    T, D = h.shape
    main_cols = 3 * CONV_W + 4 * DN_W
    w_in_t = w_in[l].T
    xn, gc, beta = _norm1(h, norm1_w[l][None], w_in_t, main_cols, dn_A_log[l][None],
                          dn_dt_bias[l][None])
    proj = _in_proj(xn, w_in_t, main_cols)
    ya = _conv_mix(proj, conv_a_w[l], conv_a_norm_w[l][None])
    gct = gc.T.reshape(N_HEADS, T // CHUNK, CHUNK)
    yb = _gdn(proj, gc, beta, gct, dn_conv_w[l], dn_norm_w[l][None])
    h1 = _out_proj(ya, yb, w_out[l], h)

    xp, idx, gates, rank, counts = _router(h1, norm2_w[l][None], w_router[l].T, b_router[l][None])
    top_idx = idx[:, :TOP_K]
    counts = counts[0].astype(I32)
    padded = ((counts + SLOT_BLK - 1) // SLOT_BLK) * SLOT_BLK
    pend = jnp.cumsum(padded)
    pstart = pend - padded
    dest_flat = (pstart[top_idx] + rank[:, :TOP_K]).reshape(-1)
    n_slots = T * TOP_K + N_EXPERTS * SLOT_BLK
    nb = n_slots // SLOT_BLK
    nused = (pend[-1] // SLOT_BLK).reshape(1)

    nsub_e = padded // SLOT_BLK
    items_e = (nsub_e + MOE_ITEM_SUBS - 1) // MOE_ITEM_SUBS
    item_end = jnp.cumsum(items_e)
    item_first = item_end - items_e
    n_items = N_EXPERTS + nb // MOE_ITEM_SUBS
    item = jnp.arange(n_items + 1, dtype=I32)
    item_e = jnp.minimum(jnp.sum((item_end[None, :] <= item[:, None]).astype(I32), axis=1),
                         N_EXPERTS - 1)
    item_j = item - item_first[item_e]
    item_valid = item < item_end[-1]
    item_start = jnp.where(item_valid, pstart[item_e] + item_j * (MOE_ITEM_SUBS * SLOT_BLK), 0)
    item_nsub = jnp.where(item_valid,
                          jnp.minimum(nsub_e[item_e] - item_j * MOE_ITEM_SUBS, MOE_ITEM_SUBS), 0)

    xs = _dispatch(dest_flat, pstart + counts, padded - counts, nused, xp, n_slots)
    ys = _moe(item_e, item_start.astype(I32), item_nsub.astype(I32), nused, xs,
              w_gate_up[l:l + 1], b_gate_up[l], w_down[l:l + 1], b_down[l])
    return h1, dest_flat, gates, ys


def kernel(x, norm1_w, w_in, conv_a_w, conv_a_norm_w, dn_conv_w, dn_A_log, dn_dt_bias, dn_norm_w,
           w_out, norm2_w, w_router, b_router, w_gate_up, b_gate_up, w_down, b_down, final_norm_w):
    B, S, D = x.shape
    depth = norm1_w.shape[0]
    assert depth == 1, "the final norm is fused into the last layer's combine"
    h = x.reshape(B * S, D)
    h1, dest_flat, gates, ys = _layer(
        h, norm1_w, w_in, conv_a_w, conv_a_norm_w, dn_conv_w, dn_A_log, dn_dt_bias, dn_norm_w,
        w_out, norm2_w, w_router, b_router, w_gate_up, b_gate_up, w_down, b_down, 0)
    out = _combine(dest_flat, h1, gates, ys, final_norm_w[None])
    return out.reshape(B, S, D)
```

```python
import functools

import jax
import jax.numpy as jnp
from jax import lax
from jax.experimental import pallas as pl
from jax.experimental.pallas import tpu as pltpu

F32 = jnp.float32
BF16 = jnp.bfloat16
I32 = jnp.int32

EPS = 1e-6
CHUNK = 64
HEAD_DIM = 128
N_HEADS = 16
CONV_W = 2048
DN_W = 2048
N_EXPERTS = 32
TOP_K = 4
EXPERT_FF = 2048
SWIGLU_LIMIT = 7.0
SWIGLU_ALPHA = 1.702

LANE = 128
VMEM_LIMIT = 56 * 1024 * 1024
SLOT_BLK = 128
MOE_SUB = SLOT_BLK
MOE_BLOCK_UNITS = 4
MOE_ITEM_SUBS = 10
MOE_DMA_SPLIT = 4
DMA_ISSUE_UNROLL = 4


def _cparams(sem):
    return pltpu.CompilerParams(dimension_semantics=sem, vmem_limit_bytes=VMEM_LIMIT)


def _mm(a, b):
    return jnp.dot(a.astype(BF16), b.astype(BF16), preferred_element_type=F32)


def _dot_nt(a, b):
    return lax.dot_general(a, b, (((1,), (1,)), ((), ())), preferred_element_type=F32)


def _mm_nt(a, b):
    return _dot_nt(a.astype(BF16), b.astype(BF16))


def _mm_tn(a, b):
    return lax.dot_general(a.astype(BF16), b.astype(BF16), (((0,), (0,)), ((), ())),
                           preferred_element_type=F32)


def _mm3_nt(x, wt):
    xh = x.astype(BF16)
    xl = (x - xh.astype(F32)).astype(BF16)
    wh = wt.astype(BF16)
    wl = (wt - wh.astype(F32)).astype(BF16)
    return _dot_nt(xh, wh) + _dot_nt(xl, wh) + _dot_nt(xh, wl)


def _sigmoid(x):
    return 1.0 / (1.0 + jnp.exp(-x))


def _norm1_kernel(x_ref, nw_ref, wab_ref, alog_ref, dtb_ref, xn_ref, gc_ref, beta_ref):
    x = x_ref[...]
    ms = jnp.mean(x * x, axis=-1, keepdims=True)
    xn = x * lax.rsqrt(ms + EPS) * nw_ref[...]
    xn_ref[...] = xn.astype(BF16)
    ab = _mm3_nt(xn, wab_ref[...])
    a = ab[:, :N_HEADS] + dtb_ref[...]
    b = ab[:, N_HEADS:]
    softplus = jnp.maximum(a, 0.0) + jnp.log1p(jnp.exp(-jnp.abs(a)))
    g = -jnp.exp(alog_ref[...]) * softplus
    pos = lax.broadcasted_iota(I32, g.shape, 0) % CHUNK
    shift = 1
    while shift < CHUNK:
        g = g + jnp.where(pos >= shift, pltpu.roll(g, shift, 0), 0.0)
        shift *= 2
    gc_ref[...] = g
    beta_ref[...] = _sigmoid(b)


def _norm1(x, nw, w_in_t, ab_row, alog, dtb, bm=256):
    T, D = x.shape
    nab = 2 * N_HEADS
    return pl.pallas_call(
        _norm1_kernel,
        grid=(T // bm,),
        in_specs=[pl.BlockSpec((bm, D), lambda i: (i, 0)),
                  pl.BlockSpec((1, D), lambda i: (0, 0)),
                  pl.BlockSpec((nab, D), lambda i: (ab_row // nab, 0)),
                  pl.BlockSpec((1, N_HEADS), lambda i: (0, 0)),
                  pl.BlockSpec((1, N_HEADS), lambda i: (0, 0))],
        out_specs=[pl.BlockSpec((bm, D), lambda i: (i, 0)),
                   pl.BlockSpec((bm, N_HEADS), lambda i: (i, 0)),
                   pl.BlockSpec((bm, N_HEADS), lambda i: (i, 0))],
        out_shape=[jax.ShapeDtypeStruct((T, D), BF16),
                   jax.ShapeDtypeStruct((T, N_HEADS), F32),
                   jax.ShapeDtypeStruct((T, N_HEADS), F32)],
        compiler_params=_cparams(("arbitrary",)),
        name="norm1",
    )(x, nw, w_in_t, alog, dtb)


def _in_proj_kernel(x_ref, w_ref, o_ref):
    o_ref[...] = _dot_nt(x_ref[...], w_ref[...].astype(BF16))


def _in_proj(xn, w_in_t, n_cols, bm=512, bn=1024):
    T, D = xn.shape
    return pl.pallas_call(
        _in_proj_kernel,
        grid=(n_cols // bn, T // bm),
        in_specs=[pl.BlockSpec((bm, D), lambda j, i: (i, 0)),
                  pl.BlockSpec((bn, D), lambda j, i: (j, 0))],
        out_specs=pl.BlockSpec((bm, bn), lambda j, i: (i, j)),
        out_shape=jax.ShapeDtypeStruct((T, n_cols), F32),
        compiler_params=_cparams(("arbitrary", "arbitrary")),
        name="in_proj",
    )(xn, w_in_t)


def _conv_mix_kernel(x_ref, b_ref, c_ref, xh_ref, ch_ref, cw_ref, nw_ref, o_ref):
    bm, bc = x_ref.shape
    u = c_ref[...] * x_ref[...]
    halo = jnp.where(pl.program_id(0) > 0, ch_ref[...] * xh_ref[...], 0.0)
    ext = jnp.concatenate([halo, u], axis=0)
    cw = cw_ref[...]
    conv = ext[6:6 + bm] * cw[0:1] + ext[7:7 + bm] * cw[1:2] + u * cw[2:3]
    y = b_ref[...] * conv
    nw = nw_ref[...]
    for g in range(bc // LANE):
        cs = slice(g * LANE, (g + 1) * LANE)
        yg = y[:, cs]
        ms = jnp.mean(yg * yg, axis=-1, keepdims=True)
        o_ref[:, cs] = (yg * lax.rsqrt(ms + EPS) * nw[:, cs]).astype(BF16)


def _conv_mix(proj, conv_w, norm_w, bm=512, bc=512):
    T = proj.shape[0]
    nc = CONV_W // bc
    hb = bm // 8
    halo_map = lambda off: (lambda i, j: (jnp.maximum(i * hb - 1, 0), j + off))
    return pl.pallas_call(
        _conv_mix_kernel,
        grid=(T // bm, nc),
        in_specs=[pl.BlockSpec((bm, bc), lambda i, j: (i, j)),
                  pl.BlockSpec((bm, bc), lambda i, j: (i, j + nc)),
                  pl.BlockSpec((bm, bc), lambda i, j: (i, j + 2 * nc)),
                  pl.BlockSpec((8, bc), halo_map(0)),
                  pl.BlockSpec((8, bc), halo_map(2 * nc)),
                  pl.BlockSpec((3, bc), lambda i, j: (0, j)),
                  pl.BlockSpec((1, bc), lambda i, j: (0, j))],
        out_specs=pl.BlockSpec((bm, bc), lambda i, j: (i, j)),
        out_shape=jax.ShapeDtypeStruct((T, CONV_W), BF16),
        compiler_params=_cparams(("arbitrary", "arbitrary")),
        name="conv_mix",
    )(proj, proj, proj, proj, proj, conv_w, norm_w)


def _gdn_kernel(q_ref, k_ref, v_ref, z_ref, gc_ref, beta_ref, gct_ref, cw_ref, nw_ref,
                y_ref, qs_ref, ks_ref, vs_ref, halo_ref, s_ref):
    tb = q_ref.shape[0]
    t = pl.program_id(0)

    @pl.when(t == 0)
    def _():
        halo_ref[...] = jnp.zeros(halo_ref.shape, F32)
        s_ref[...] = jnp.zeros(s_ref.shape, F32)

    for a, (src, dst) in enumerate(((q_ref, qs_ref), (k_ref, ks_ref), (v_ref, vs_ref))):
        for h in range(N_HEADS):
            cs = slice(h * HEAD_DIM, (h + 1) * HEAD_DIM)
            raw = src[:, cs]
            ext = jnp.concatenate([halo_ref[a, :, cs], raw], axis=0)
            cw = cw_ref[:, a * DN_W + h * HEAD_DIM:a * DN_W + (h + 1) * HEAD_DIM]
            c = (ext[5:5 + tb] * cw[0:1] + ext[6:6 + tb] * cw[1:2]
                 + ext[7:7 + tb] * cw[2:3] + raw * cw[3:4])
            c = c * _sigmoid(c)
            if a < 2:
                c = c * lax.rsqrt(jnp.sum(c * c, axis=-1, keepdims=True) + EPS)
            if a == 0:
                c = c * (HEAD_DIM ** -0.5)
            dst[:, cs] = c
            halo_ref[a, :, cs] = raw[tb - 8:tb]

    ii = lax.broadcasted_iota(I32, (CHUNK, CHUNK), 0)
    jj = lax.broadcasted_iota(I32, (CHUNK, CHUNK), 1)
    eye = (ii == jj).astype(F32)
    nw = nw_ref[...]

    heads = range(N_HEADS)
    cols = [slice(h * HEAD_DIM, (h + 1) * HEAD_DIM) for h in heads]

    def chunk(c, carry):
        rows = pl.ds(pl.multiple_of(c * CHUNK, CHUNK), CHUNK)
        cg = t * (tb // CHUNK) + c
        gcb = gc_ref[rows, :]
        btb = beta_ref[rows, :]
        q = [qs_ref[rows, cs] for cs in cols]
        k = [ks_ref[rows, cs] for cs in cols]
        v = [vs_ref[rows, cs] for cs in cols]
        z = [z_ref[rows, cs] for cs in cols]
        s = [s_ref[h] for h in heads]
        gcol = [gcb[:, h:h + 1] for h in heads]
        bcol = [btb[:, h:h + 1] for h in heads]
        grow = [gct_ref[h, pl.ds(cg, 1), :] for h in heads]
        gl = [g[:, CHUNK - 1:CHUNK] for g in grow]
        eg = [jnp.exp(g) for g in gcol]
        kb = [k[h] * bcol[h] for h in heads]
        a1 = [_mm_nt(jnp.concatenate([kb[h], q[h]], axis=0), k[h]) for h in heads]
        dec = [jnp.where(ii >= jj, jnp.exp(jnp.minimum(gcol[h] - grow[h], 0.0)), 0.0) for h in heads]
        lmat = [jnp.where(ii > jj, a1[h][:CHUNK] * dec[h], 0.0) for h in heads]
        qkd = [a1[h][CHUNK:] * dec[h] for h in heads]
        p = [eye - lmat[h] for h in heads]
        cur = [_mm(lmat[h], lmat[h]) for h in heads]
        power = 2
        while 2 * power < CHUNK:
            r = [_mm(jnp.concatenate([p[h], cur[h]], axis=0), cur[h]) for h in heads]
            p = [p[h] + r[h][:CHUNK] for h in heads]
            cur = [r[h][CHUNK:] for h in heads]
            power *= 2
        p = [p[h] + _mm(p[h], cur[h]) for h in heads]
        uw = [_mm(p[h], jnp.concatenate([v[h] * bcol[h], kb[h] * eg[h]], axis=1)) for h in heads]
        ws = [_mm(jnp.concatenate([uw[h][:, HEAD_DIM:], q[h] * eg[h]], axis=0), s[h]) for h in heads]
        vn = [uw[h][:, :HEAD_DIM] - ws[h][:CHUNK] for h in heads]
        o = [ws[h][CHUNK:] + _mm(qkd[h], vn[h]) for h in heads]
        kt = [k[h] * jnp.exp(gl[h] - gcol[h]) for h in heads]
        s_new = [s[h] * jnp.exp(gl[h]) + _mm_tn(kt[h], vn[h]) for h in heads]
        for h in heads:
            on = o[h] * lax.rsqrt(jnp.mean(o[h] * o[h], axis=-1, keepdims=True) + EPS) * nw
            y_ref[rows, cols[h]] = (on * (z[h] * _sigmoid(z[h]))).astype(BF16)
            s_ref[h] = s_new[h]
        return carry

    lax.fori_loop(0, tb // CHUNK, chunk, 0)


def _gdn(proj, gc, beta, gct, conv_w, norm_w, tb=256):
    T = proj.shape[0]
    qb = (3 * CONV_W) // DN_W
    n_chunks = T // CHUNK
    return pl.pallas_call(
        _gdn_kernel,
        grid=(T // tb,),
        in_specs=[pl.BlockSpec((tb, DN_W), lambda t: (t, qb)),
                  pl.BlockSpec((tb, DN_W), lambda t: (t, qb + 1)),
                  pl.BlockSpec((tb, DN_W), lambda t: (t, qb + 2)),
                  pl.BlockSpec((tb, DN_W), lambda t: (t, qb + 3)),
                  pl.BlockSpec((tb, N_HEADS), lambda t: (t, 0)),
                  pl.BlockSpec((tb, N_HEADS), lambda t: (t, 0)),
                  pl.BlockSpec((N_HEADS, n_chunks, CHUNK), lambda t: (0, 0, 0)),
                  pl.BlockSpec((4, 3 * DN_W), lambda t: (0, 0)),
                  pl.BlockSpec((1, HEAD_DIM), lambda t: (0, 0))],
        out_specs=pl.BlockSpec((tb, DN_W), lambda t: (t, 0)),
        out_shape=jax.ShapeDtypeStruct((T, DN_W), BF16),
        scratch_shapes=[pltpu.VMEM((tb, DN_W), F32),
                        pltpu.VMEM((tb, DN_W), F32),
                        pltpu.VMEM((tb, DN_W), F32),
                        pltpu.VMEM((3, 8, DN_W), F32),
                        pltpu.VMEM((N_HEADS, HEAD_DIM, HEAD_DIM), F32)],
        compiler_params=_cparams(("arbitrary",)),
        name="gdn",
    )(proj, proj, proj, proj, gc, beta, gct, conv_w, norm_w)


def _out_proj_kernel(ya_ref, yb_ref, w_ref, x_ref, o_ref):
    o_ref[...] = (x_ref[...]
                  + jnp.dot(ya_ref[...], w_ref[:CONV_W].astype(BF16), preferred_element_type=F32)
                  + jnp.dot(yb_ref[...], w_ref[CONV_W:].astype(BF16), preferred_element_type=F32))


def _out_proj(ya, yb, w_out, x, bm=512, bn=1024):
    T, D = x.shape
    K = w_out.shape[0]
    return pl.pallas_call(
        _out_proj_kernel,
        grid=(D // bn, T // bm),
        in_specs=[pl.BlockSpec((bm, CONV_W), lambda j, i: (i, 0)),
                  pl.BlockSpec((bm, DN_W), lambda j, i: (i, 0)),
                  pl.BlockSpec((K, bn), lambda j, i: (0, j)),
                  pl.BlockSpec((bm, bn), lambda j, i: (i, j))],
        out_specs=pl.BlockSpec((bm, bn), lambda j, i: (i, j)),
        out_shape=jax.ShapeDtypeStruct((T, D), F32),
        compiler_params=_cparams(("arbitrary", "arbitrary")),
        name="out_proj",
    )(ya, yb, w_out, x)


def _router_kernel(h_ref, nw_ref, wr_ref, br_ref, xp_ref, idx_ref, gate_ref, rank_ref, cnt_ref,
                   carry_ref):
    bm, D = h_ref.shape

    @pl.when(pl.program_id(0) == 0)
    def _():
        carry_ref[...] = jnp.zeros(carry_ref.shape, F32)

    h = h_ref[...]
    ms = jnp.mean(h * h, axis=-1, keepdims=True)
    xn = h * lax.rsqrt(ms + EPS) * nw_ref[...]
    xp_ref[...] = pltpu.pack_elementwise([xn[:, :D // 2], xn[:, D // 2:]], packed_dtype=BF16)

    logits = _mm3_nt(xn, wr_ref[...]) + br_ref[...]
    lane_e = lax.broadcasted_iota(I32, logits.shape, 1)
    work = logits
    sels, vals, idxs = [], [], []
    for _ in range(TOP_K):
        m = jnp.max(work, axis=-1, keepdims=True)
        idx = jnp.min(jnp.where(work == m, lane_e, N_EXPERTS), axis=-1, keepdims=True)
        sel = lane_e == idx
        work = jnp.where(sel, -jnp.inf, work)
        sels.append(sel)
        vals.append(m)
        idxs.append(idx)
    exps = [jnp.exp(v - vals[0]) for v in vals]
    denom = exps[0] + exps[1] + exps[2] + exps[3]
    gates = [e / denom for e in exps]

    onehot = jnp.zeros(logits.shape, F32)
    for sel in sels:
        onehot = onehot + sel.astype(F32)
    ri = lax.broadcasted_iota(I32, (bm, bm), 0)
    ci = lax.broadcasted_iota(I32, (bm, bm), 1)
    tri = (ri > ci).astype(BF16)
    carry = carry_ref[...]
    before = jnp.dot(tri, onehot.astype(BF16), preferred_element_type=F32) + carry
    ranks = [jnp.sum(jnp.where(sel, before, 0.0), axis=-1, keepdims=True).astype(I32)
             for sel in sels]
    new_carry = carry + jnp.sum(onehot, axis=0, keepdims=True)
    carry_ref[...] = new_carry
    cnt_ref[...] = new_carry

    lane = lax.broadcasted_iota(I32, (bm, LANE), 1)

    def spread(cols):
        return jnp.where(lane == 0, cols[0],
                         jnp.where(lane == 1, cols[1], jnp.where(lane == 2, cols[2], cols[3])))

    idx_ref[...] = spread(idxs)
    gate_ref[...] = spread(gates)
    rank_ref[...] = spread(ranks)


def _router(h1, nw, w_router_t, b_router, bm=256):
    T, D = h1.shape
    return pl.pallas_call(
        _router_kernel,
        grid=(T // bm,),
        in_specs=[pl.BlockSpec((bm, D), lambda i: (i, 0)),
                  pl.BlockSpec((1, D), lambda i: (0, 0)),
                  pl.BlockSpec((N_EXPERTS, D), lambda i: (0, 0)),
                  pl.BlockSpec((1, N_EXPERTS), lambda i: (0, 0))],
        out_specs=[pl.BlockSpec((bm, D // 2), lambda i: (i, 0)),
                   pl.BlockSpec((bm, LANE), lambda i: (i, 0)),
                   pl.BlockSpec((bm, LANE), lambda i: (i, 0)),
                   pl.BlockSpec((bm, LANE), lambda i: (i, 0)),
                   pl.BlockSpec((1, N_EXPERTS), lambda i: (0, 0))],
        out_shape=[jax.ShapeDtypeStruct((T, D // 2), jnp.uint32),
                   jax.ShapeDtypeStruct((T, LANE), I32),
                   jax.ShapeDtypeStruct((T, LANE), F32),
                   jax.ShapeDtypeStruct((T, LANE), I32),
                   jax.ShapeDtypeStruct((1, N_EXPERTS), F32)],
        scratch_shapes=[pltpu.VMEM((1, N_EXPERTS), F32)],
        compiler_params=_cparams(("arbitrary",)),
        name="router",
    )(h1, nw, w_router_t, b_router)


def _dispatch_kernel(dest_ref, pad_start_ref, pad_len_ref, nused_ref, x_ref, xs_ref, zero_ref, sem):
    tb = x_ref.shape[0]
    i = pl.program_id(0)
    n_blocks = xs_ref.shape[0] // SLOT_BLK

    def row_copy(r, slot):
        return pltpu.make_async_copy(x_ref.at[pl.ds(r, 1)], xs_ref.at[pl.ds(slot, 1)], sem)

    def zero_row_copy(slot):
        return pltpu.make_async_copy(zero_ref.at[pl.ds(0, 1)], xs_ref.at[pl.ds(slot, 1)], sem)

    def zero_blk_copy(b):
        start = pl.multiple_of(b * SLOT_BLK, SLOT_BLK)
        return pltpu.make_async_copy(zero_ref, xs_ref.at[pl.ds(start, SLOT_BLK)], sem)

    @pl.when(i == 0)
    def _():
        zero_ref[...] = jnp.zeros(zero_ref.shape, zero_ref.dtype)

        def per_expert(e, c):
            start = pad_start_ref[e]
            n = pad_len_ref[e]
            lax.fori_loop(0, n, lambda j, cc: (zero_row_copy(start + j).start(), cc)[1], 0)
            lax.fori_loop(0, n, lambda j, cc: (zero_row_copy(start + j).wait(), cc)[1], 0)
            return c

        lax.fori_loop(0, N_EXPERTS, per_expert, 0)
        nu = nused_ref[0]
        lax.fori_loop(nu, n_blocks, lambda b, cc: (zero_blk_copy(b).start(), cc)[1], 0)
        lax.fori_loop(nu, n_blocks, lambda b, cc: (zero_blk_copy(b).wait(), cc)[1], 0)

    def issue(r, c):
        base = (i * tb + r) * TOP_K
        for k in range(TOP_K):
            row_copy(r, dest_ref[base + k]).start()
        return c

    def drain(r, c):
        for k in range(TOP_K):
            row_copy(0, 0).wait()
        return c

    lax.fori_loop(0, tb, issue, 0, unroll=DMA_ISSUE_UNROLL)
    lax.fori_loop(0, tb, drain, 0, unroll=DMA_ISSUE_UNROLL)


def _dispatch(dest_flat, pad_start, pad_len, nused, xp, n_slots, tb=256):
    T, W = xp.shape
    return pl.pallas_call(
        _dispatch_kernel,
        grid_spec=pltpu.PrefetchScalarGridSpec(
            num_scalar_prefetch=4,
            grid=(T // tb,),
            in_specs=[pl.BlockSpec((tb, W), lambda i, *_: (i, 0))],
            out_specs=pl.BlockSpec(memory_space=pl.ANY),
            scratch_shapes=[pltpu.VMEM((SLOT_BLK, W), xp.dtype),
                            pltpu.SemaphoreType.DMA(())],
        ),
        out_shape=jax.ShapeDtypeStruct((n_slots, W), xp.dtype),
        compiler_params=_cparams(("arbitrary",)),
        name="dispatch",
    )(dest_flat, pad_start, pad_len, nused, xp)


def _unpack_pair(xw):
    xa = pltpu.unpack_elementwise(xw, index=0, packed_dtype=BF16, unpacked_dtype=F32)
    xb = pltpu.unpack_elementwise(xw, index=1, packed_dtype=BF16, unpacked_dtype=F32)
    return xa.astype(BF16), xb.astype(BF16)


def _moe_kernel(item_e_ref, item_start_ref, item_nsub_ref, nused_ref,
                xs_ref, wgu_ref, wd_ref, bgu_ref, bd_ref, ys_ref,
                x_ref, h_ref, g_ref, gu_ref, d_ref, stage_ref, pend_ref,
                sem_x, sem_gu, sem_d, sem_o):
    i = pl.program_id(0)
    tf = gu_ref.shape[-1]
    tn = d_ref.shape[-1]
    half = gu_ref.shape[1] // 2
    nf = EXPERT_FF // tf
    nn = wd_ref.shape[-1] // tn
    n_blocks = ys_ref.shape[0] // MOE_SUB
    GATE, UP = 0, 1

    e = item_e_ref[i]
    start = pl.multiple_of(item_start_ref[i], MOE_SUB)
    nsub = item_nsub_ref[i]
    nxt_e = item_e_ref[i + 1]
    nxt_start = pl.multiple_of(item_start_ref[i + 1], MOE_SUB)
    nxt_nsub = item_nsub_ref[i + 1]

    gu_rows = gu_ref.shape[1] // MOE_DMA_SPLIT
    d_rows = d_ref.shape[1] // MOE_DMA_SPLIT

    def gu_copies(ee, part, f, slot=None):
        slot = part if slot is None else slot
        col0 = part * EXPERT_FF + pl.multiple_of(f * tf, tf)
        return [pltpu.make_async_copy(
            wgu_ref.at[0, ee, pl.ds(s * gu_rows, gu_rows), pl.ds(col0, tf)],
            gu_ref.at[slot, pl.ds(s * gu_rows, gu_rows)], sem_gu.at[slot])
            for s in range(MOE_DMA_SPLIT)]

    def d_copies(ee, n, slot):
        return [pltpu.make_async_copy(
            wd_ref.at[0, ee, pl.ds(s * d_rows, d_rows), pl.ds(pl.multiple_of(n * tn, tn), tn)],
            d_ref.at[slot, pl.ds(s * d_rows, d_rows)], sem_d.at[slot])
            for s in range(MOE_DMA_SPLIT)]

    def x_copy(row0, r):
        off = pl.multiple_of(r * MOE_SUB, MOE_SUB)
        return pltpu.make_async_copy(xs_ref.at[pl.ds(row0 + off, MOE_SUB)],
                                     x_ref.at[pl.ds(off, MOE_SUB)], sem_x)

    def out_copy(row0, nrows, n, slot):
        return pltpu.make_async_copy(
            stage_ref.at[slot, pl.ds(0, nrows)],
            ys_ref.at[pl.ds(row0, nrows), pl.ds(pl.multiple_of(n * tn, tn), tn)], sem_o.at[slot])

    def wait_out(slot):
        for units in (1, 2, MOE_BLOCK_UNITS):
            @pl.when(pend_ref[slot] == units)
            def _(units=units):
                out_copy(start, units * MOE_SUB, 0, slot).wait()

    def for_row_blocks(body):
        nfull = nsub // MOE_BLOCK_UNITS
        rem = nsub % MOE_BLOCK_UNITS
        full_rows = MOE_BLOCK_UNITS * MOE_SUB

        def full(p, c):
            body(p, pl.multiple_of(p * full_rows, full_rows), full_rows)
            return c

        lax.fori_loop(0, nfull, full, 0)
        has2 = (rem // 2) % 2

        @pl.when(has2 == 1)
        def _():
            body(nfull, pl.multiple_of(nfull * full_rows, full_rows), 2 * MOE_SUB)

        @pl.when(rem % 2 == 1)
        def _():
            body(nfull + has2,
                 pl.multiple_of(nfull * full_rows + has2 * 2 * MOE_SUB, MOE_SUB), MOE_SUB)

    def start_gu(ee, part, f):
        for cp in gu_copies(ee, part, f):
            cp.start()

    def wait_gu(part):
        for cp in gu_copies(e, part, 0):
            cp.wait()

    def start_d(ee, n, slot):
        for cp in d_copies(ee, n, slot):
            cp.start()

    def load_rows(row0, count):
        lax.fori_loop(0, count, lambda r, c: (x_copy(row0, r).start(), c)[1], 0)

    @pl.when(i == 0)
    def _():
        pend_ref[0] = 0
        pend_ref[1] = 0
        stage_ref[0] = jnp.zeros(stage_ref.shape[1:], F32)

        def zero_block(b, c):
            row0 = pl.multiple_of(b * MOE_SUB, MOE_SUB)
            for n in range(nn):
                out_copy(row0, MOE_SUB, n, 0).start()
            for n in range(nn):
                out_copy(row0, MOE_SUB, n, 0).wait()
            return c

        lax.fori_loop(nused_ref[0], n_blocks, zero_block, 0)

        @pl.when(nsub > 0)
        def _():
            load_rows(start, nsub)
            start_gu(e, GATE, 0)

    def x_dot(rows, slot):
        xa, xb = _unpack_pair(x_ref[rows, :])
        return (jnp.dot(xa, gu_ref[slot, :half].astype(BF16), preferred_element_type=F32)
                + jnp.dot(xb, gu_ref[slot, half:].astype(BF16), preferred_element_type=F32))

    def gate_up_tiles(f, c):
        wait_gu(GATE)
        start_gu(e, UP, f)
        bias_g = bgu_ref[e, pl.ds(f, 1), :]

        def gate_rows(blk, off, nrows):
            rows = pl.ds(off, nrows)
            g_ref[rows, :] = jnp.minimum(x_dot(rows, GATE) + bias_g, SWIGLU_LIMIT)

        for_row_blocks(gate_rows)

        wait_gu(UP)

        @pl.when(f + 1 < nf)
        def _():
            start_gu(e, GATE, f + 1)

        @pl.when(f + 1 == nf)
        def _():
            start_d(e, 0, 0)

        @pl.when(jnp.logical_and(f + 1 == nf, nxt_nsub > 0))
        def _():
            start_gu(nxt_e, GATE, 0)

        bias_u = bgu_ref[e, pl.ds(nf + f, 1), :]

        def up_rows(blk, off, nrows):
            rows = pl.ds(off, nrows)
            gate = g_ref[rows, :]
            up = jnp.clip(x_dot(rows, UP) + bias_u, -SWIGLU_LIMIT, SWIGLU_LIMIT)
            glu = gate * _sigmoid(SWIGLU_ALPHA * gate)
            h_ref[f, rows, :] = ((up + 1.0) * glu).astype(BF16)

        for_row_blocks(up_rows)
        return c

    def down_tiles(n, c):
        slot = n % 2
        for cp in d_copies(e, n, slot):
            cp.wait()

        @pl.when(n + 1 < nn)
        def _():
            start_d(e, n + 1, 1 - slot)

        @pl.when(jnp.logical_and(n == 0, nxt_nsub > 0))
        def _():
            load_rows(nxt_start, nxt_nsub)

        bias_d = bd_ref[e, pl.ds(n, 1), :]

        def down_rows(blk, off, nrows):
            os = blk % 2
            wait_out(os)
            rows = pl.ds(off, nrows)
            acc = bias_d
            for f in range(nf):
                acc = acc + jnp.dot(h_ref[f, rows, :], d_ref[slot, f * tf:(f + 1) * tf, :].astype(BF16),
                                    preferred_element_type=F32)
            stage_ref[os, pl.ds(0, nrows)] = acc
            out_copy(start + off, nrows, n, os).start()
            pend_ref[os] = nrows // MOE_SUB

        for_row_blocks(down_rows)
        return c

    @pl.when(nsub > 0)
    def _():
        lax.fori_loop(0, nsub, lambda r, c: (x_copy(start, r).wait(), c)[1], 0)
        lax.fori_loop(0, nf, gate_up_tiles, 0)
        lax.fori_loop(0, nn, down_tiles, 0)
        for os in range(2):
            wait_out(os)
            pend_ref[os] = 0


def _moe(item_e, item_start, item_nsub, nused, xs, w_gate_up, b_gate_up, w_down, b_down,
         tf=512, tn=512):
    n_slots, W = xs.shape
    D = 2 * W
    n_items = item_e.shape[0] - 1
    rb = MOE_ITEM_SUBS * MOE_SUB
    b_gate_up = b_gate_up.reshape(N_EXPERTS, 2 * EXPERT_FF // tf, tf)
    b_down = b_down.reshape(N_EXPERTS, D // tn, tn)
    any_spec = pl.BlockSpec(memory_space=pl.ANY)
    return pl.pallas_call(
        _moe_kernel,
        grid_spec=pltpu.PrefetchScalarGridSpec(
            num_scalar_prefetch=4,
            grid=(n_items,),
            in_specs=[any_spec, any_spec, any_spec,
                      pl.BlockSpec(b_gate_up.shape, lambda i, *_: (0, 0, 0)),
                      pl.BlockSpec(b_down.shape, lambda i, *_: (0, 0, 0))],
            out_specs=any_spec,
            scratch_shapes=[pltpu.VMEM((rb, W), xs.dtype),
                            pltpu.VMEM((EXPERT_FF // tf, rb, tf), BF16),
                            pltpu.VMEM((rb, tf), F32),
                            pltpu.VMEM((2, D, tf), F32),
                            pltpu.VMEM((2, EXPERT_FF, tn), F32),
                            pltpu.VMEM((2, MOE_BLOCK_UNITS * MOE_SUB, tn), F32),
                            pltpu.SMEM((2,), I32),
                            pltpu.SemaphoreType.DMA(()),
                            pltpu.SemaphoreType.DMA((2,)),
                            pltpu.SemaphoreType.DMA((2,)),
                            pltpu.SemaphoreType.DMA((2,))],
        ),
        out_shape=jax.ShapeDtypeStruct((n_slots, D), F32),
        compiler_params=_cparams(("arbitrary",)),
        name="moe",
    )(item_e, item_start, item_nsub, nused, xs, w_gate_up, w_down, b_gate_up, b_down)


def _combine_kernel(dest_ref, h_ref, gate_ref, ys_ref, nw_ref, o_ref, buf_ref, sem):
    tb = h_ref.shape[0]
    i = pl.program_id(0)
    n_steps = pl.num_programs(0)

    def row_copy(slot, buf, k, r):
        return pltpu.make_async_copy(ys_ref.at[pl.ds(slot, 1)], buf_ref.at[buf, k, pl.ds(r, 1)],
                                     sem.at[buf])

    def gather(step, buf):
        def issue(r, c):
            base = (step * tb + r) * TOP_K
            for k in range(TOP_K):
                row_copy(dest_ref[base + k], buf, k, r).start()
            return c

        lax.fori_loop(0, tb, issue, 0, unroll=DMA_ISSUE_UNROLL)

    @pl.when(i == 0)
    def _():
        gather(0, 0)

    cur = i % 2

    @pl.when(i + 1 < n_steps)
    def _():
        gather(i + 1, 1 - cur)

    def drain(r, c):
        for k in range(TOP_K):
            row_copy(0, cur, k, r).wait()
        return c

    lax.fori_loop(0, tb, drain, 0, unroll=DMA_ISSUE_UNROLL)

    gates = gate_ref[...]
    h = h_ref[...]
    for k in range(TOP_K):
        h = h + buf_ref[cur, k] * gates[:, k:k + 1]
    ms = jnp.mean(h * h, axis=-1, keepdims=True)
    o_ref[...] = h * lax.rsqrt(ms + EPS) * nw_ref[...]


def _combine(dest_flat, h1, gates, ys, nw, tb=128):
    T, D = h1.shape
    return pl.pallas_call(
        _combine_kernel,
        grid_spec=pltpu.PrefetchScalarGridSpec(
            num_scalar_prefetch=1,
            grid=(T // tb,),
            in_specs=[pl.BlockSpec((tb, D), lambda i, *_: (i, 0)),
                      pl.BlockSpec((tb, LANE), lambda i, *_: (i, 0)),
                      pl.BlockSpec(memory_space=pl.ANY),
                      pl.BlockSpec((1, D), lambda i, *_: (0, 0))],
            out_specs=pl.BlockSpec((tb, D), lambda i, *_: (i, 0)),
            scratch_shapes=[pltpu.VMEM((2, TOP_K, tb, D), F32),
                            pltpu.SemaphoreType.DMA((2,))],
        ),
        out_shape=jax.ShapeDtypeStruct((T, D), F32),
        compiler_params=_cparams(("arbitrary",)),
        name="combine",
    )(dest_flat, h1, gates, ys, nw)


def _layer(h, norm1_w, w_in, conv_a_w, conv_a_norm_w, dn_conv_w, dn_A_log, dn_dt_bias, dn_norm_w,
           w_out, norm2_w, w_router, b_router, w_gate_up, b_gate_up, w_down, b_down, l):
    T, D = h.shape
    main_cols = 3 * CONV_W + 4 * DN_W
    w_in_t = w_in[l].T
    xn, gc, beta = _norm1(h, norm1_w[l][None], w_in_t, main_cols, dn_A_log[l][None],
                          dn_dt_bias[l][None])
    proj = _in_proj(xn, w_in_t, main_cols)
    ya = _conv_mix(proj, conv_a_w[l], conv_a_norm_w[l][None])
    gct = gc.T.reshape(N_HEADS, T // CHUNK, CHUNK)
    yb = _gdn(proj, gc, beta, gct, dn_conv_w[l], dn_norm_w[l][None])
    h1 = _out_proj(ya, yb, w_out[l], h)

    xp, idx, gates, rank, counts = _router(h1, norm2_w[l][None], w_router[l].T, b_router[l][None])
    top_idx = idx[:, :TOP_K]
    counts = counts[0].astype(I32)
    padded = ((counts + SLOT_BLK - 1) // SLOT_BLK) * SLOT_BLK
    pend = jnp.cumsum(padded)
    pstart = pend - padded
    dest_flat = (pstart[top_idx] + rank[:, :TOP_K]).reshape(-1)
    n_slots = T * TOP_K + N_EXPERTS * SLOT_BLK
    nb = n_slots // SLOT_BLK
    nused = (pend[-1] // SLOT_BLK).reshape(1)

    nsub_e = padded // SLOT_BLK
    items_e = (nsub_e + MOE_ITEM_SUBS - 1) // MOE_ITEM_SUBS
    item_end = jnp.cumsum(items_e)
    item_first = item_end - items_e
    n_items = N_EXPERTS + nb // MOE_ITEM_SUBS
    item = jnp.arange(n_items + 1, dtype=I32)
    item_e = jnp.minimum(jnp.sum((item_end[None, :] <= item[:, None]).astype(I32), axis=1),
                         N_EXPERTS - 1)
    item_j = item - item_first[item_e]
    item_valid = item < item_end[-1]
    item_start = jnp.where(item_valid, pstart[item_e] + item_j * (MOE_ITEM_SUBS * SLOT_BLK), 0)
    item_nsub = jnp.where(item_valid,
                          jnp.minimum(nsub_e[item_e] - item_j * MOE_ITEM_SUBS, MOE_ITEM_SUBS), 0)

    xs = _dispatch(dest_flat, pstart + counts, padded - counts, nused, xp, n_slots)
    ys = _moe(item_e, item_start.astype(I32), item_nsub.astype(I32), nused, xs,
              w_gate_up[l:l + 1], b_gate_up[l], w_down[l:l + 1], b_down[l])
    return h1, dest_flat, gates, ys


def kernel(x, norm1_w, w_in, conv_a_w, conv_a_norm_w, dn_conv_w, dn_A_log, dn_dt_bias, dn_norm_w,
           w_out, norm2_w, w_router, b_router, w_gate_up, b_gate_up, w_down, b_down, final_norm_w):
    B, S, D = x.shape
    depth = norm1_w.shape[0]
    assert depth == 1, "the final norm is fused into the last layer's combine"
    h = x.reshape(B * S, D)
    h1, dest_flat, gates, ys = _layer(
        h, norm1_w, w_in, conv_a_w, conv_a_norm_w, dn_conv_w, dn_A_log, dn_dt_bias, dn_norm_w,
        w_out, norm2_w, w_router, b_router, w_gate_up, b_gate_up, w_down, b_down, 0)
    out = _combine(dest_flat, h1, gates, ys, final_norm_w[None])
    return out.reshape(B, S, D)
```

```python
import functools

import jax
import jax.numpy as jnp
from jax import lax
from jax.experimental import pallas as pl
from jax.experimental.pallas import tpu as pltpu

F32 = jnp.float32
BF16 = jnp.bfloat16
I32 = jnp.int32

EPS = 1e-6
CHUNK = 64
HEAD_DIM = 128
N_HEADS = 16
CONV_W = 2048
DN_W = 2048
N_EXPERTS = 32
TOP_K = 4
EXPERT_FF = 2048
SWIGLU_LIMIT = 7.0
SWIGLU_ALPHA = 1.702

LANE = 128
VMEM_LIMIT = 56 * 1024 * 1024
SLOT_BLK = 128
MOE_SUB = SLOT_BLK
MOE_BLOCK_UNITS = 4
MOE_ITEM_SUBS = 10
MOE_TF = 512
MOE_TN = 512
MOE_DMA_SPLIT = 4
DMA_ISSUE_UNROLL = 4


def _cparams(sem):
    return pltpu.CompilerParams(dimension_semantics=sem, vmem_limit_bytes=VMEM_LIMIT)


def _mm(a, b):
    return jnp.dot(a.astype(BF16), b.astype(BF16), preferred_element_type=F32)


def _dot_nt(a, b):
    return lax.dot_general(a, b, (((1,), (1,)), ((), ())), preferred_element_type=F32)


def _mm_nt(a, b):
    return _dot_nt(a.astype(BF16), b.astype(BF16))


def _mm_tn(a, b):
    return lax.dot_general(a.astype(BF16), b.astype(BF16), (((0,), (0,)), ((), ())),
                           preferred_element_type=F32)


def _mm3_nt(x, wt):
    xh = x.astype(BF16)
    xl = (x - xh.astype(F32)).astype(BF16)
    wh = wt.astype(BF16)
    wl = (wt - wh.astype(F32)).astype(BF16)
    return _dot_nt(xh, wh) + _dot_nt(xl, wh) + _dot_nt(xh, wl)


def _sigmoid(x):
    return 1.0 / (1.0 + jnp.exp(-x))


def _norm1_kernel(x_ref, nw_ref, wab_ref, alog_ref, dtb_ref, xn_ref, gc_ref, beta_ref):
    x = x_ref[...]
    ms = jnp.mean(x * x, axis=-1, keepdims=True)
    xn = x * lax.rsqrt(ms + EPS) * nw_ref[...]
    xn_ref[...] = xn.astype(BF16)
    ab = _mm3_nt(xn, wab_ref[...])
    a = ab[:, :N_HEADS] + dtb_ref[...]
    b = ab[:, N_HEADS:]
    softplus = jnp.maximum(a, 0.0) + jnp.log1p(jnp.exp(-jnp.abs(a)))
    g = -jnp.exp(alog_ref[...]) * softplus
    pos = lax.broadcasted_iota(I32, g.shape, 0) % CHUNK
    shift = 1
    while shift < CHUNK:
        g = g + jnp.where(pos >= shift, pltpu.roll(g, shift, 0), 0.0)
        shift *= 2
    gc_ref[...] = g
    beta_ref[...] = _sigmoid(b)


def _norm1(x, nw, w_in_t, ab_row, alog, dtb, bm=256):
    T, D = x.shape
    nab = 2 * N_HEADS
    return pl.pallas_call(
        _norm1_kernel,
        grid=(T // bm,),
        in_specs=[pl.BlockSpec((bm, D), lambda i: (i, 0)),
                  pl.BlockSpec((1, D), lambda i: (0, 0)),
                  pl.BlockSpec((nab, D), lambda i: (ab_row // nab, 0)),
                  pl.BlockSpec((1, N_HEADS), lambda i: (0, 0)),
                  pl.BlockSpec((1, N_HEADS), lambda i: (0, 0))],
        out_specs=[pl.BlockSpec((bm, D), lambda i: (i, 0)),
                   pl.BlockSpec((bm, N_HEADS), lambda i: (i, 0)),
                   pl.BlockSpec((bm, N_HEADS), lambda i: (i, 0))],
        out_shape=[jax.ShapeDtypeStruct((T, D), BF16),
                   jax.ShapeDtypeStruct((T, N_HEADS), F32),
                   jax.ShapeDtypeStruct((T, N_HEADS), F32)],
        compiler_params=_cparams(("arbitrary",)),
        name="norm1",
    )(x, nw, w_in_t, alog, dtb)


def _in_proj_kernel(x_ref, w_ref, o_ref):
    o_ref[...] = _dot_nt(x_ref[...], w_ref[...].astype(BF16))


def _in_proj(xn, w_in_t, n_cols, bm=512, bn=1024):
    T, D = xn.shape
    return pl.pallas_call(
        _in_proj_kernel,
        grid=(n_cols // bn, T // bm),
        in_specs=[pl.BlockSpec((bm, D), lambda j, i: (i, 0)),
                  pl.BlockSpec((bn, D), lambda j, i: (j, 0))],
        out_specs=pl.BlockSpec((bm, bn), lambda j, i: (i, j)),
        out_shape=jax.ShapeDtypeStruct((T, n_cols), F32),
        compiler_params=_cparams(("arbitrary", "arbitrary")),
        name="in_proj",
    )(xn, w_in_t)


def _conv_mix_kernel(x_ref, b_ref, c_ref, xh_ref, ch_ref, cw_ref, nw_ref, o_ref):
    bm, bc = x_ref.shape
    u = c_ref[...] * x_ref[...]
    halo = jnp.where(pl.program_id(0) > 0, ch_ref[...] * xh_ref[...], 0.0)
    ext = jnp.concatenate([halo, u], axis=0)
    cw = cw_ref[...]
    conv = ext[6:6 + bm] * cw[0:1] + ext[7:7 + bm] * cw[1:2] + u * cw[2:3]
    y = b_ref[...] * conv
    nw = nw_ref[...]
    for g in range(bc // LANE):
        cs = slice(g * LANE, (g + 1) * LANE)
        yg = y[:, cs]
        ms = jnp.mean(yg * yg, axis=-1, keepdims=True)
        o_ref[:, cs] = (yg * lax.rsqrt(ms + EPS) * nw[:, cs]).astype(BF16)


def _conv_mix(proj, conv_w, norm_w, bm=512, bc=512):
    T = proj.shape[0]
    nc = CONV_W // bc
    hb = bm // 8
    halo_map = lambda off: (lambda i, j: (jnp.maximum(i * hb - 1, 0), j + off))
    return pl.pallas_call(
        _conv_mix_kernel,
        grid=(T // bm, nc),
        in_specs=[pl.BlockSpec((bm, bc), lambda i, j: (i, j)),
                  pl.BlockSpec((bm, bc), lambda i, j: (i, j + nc)),
                  pl.BlockSpec((bm, bc), lambda i, j: (i, j + 2 * nc)),
                  pl.BlockSpec((8, bc), halo_map(0)),
                  pl.BlockSpec((8, bc), halo_map(2 * nc)),
                  pl.BlockSpec((3, bc), lambda i, j: (0, j)),
                  pl.BlockSpec((1, bc), lambda i, j: (0, j))],
        out_specs=pl.BlockSpec((bm, bc), lambda i, j: (i, j)),
        out_shape=jax.ShapeDtypeStruct((T, CONV_W), BF16),
        compiler_params=_cparams(("arbitrary", "arbitrary")),
        name="conv_mix",
    )(proj, proj, proj, proj, proj, conv_w, norm_w)


def _gdn_kernel(q_ref, k_ref, v_ref, z_ref, gc_ref, beta_ref, gct_ref, cw_ref, nw_ref,
                y_ref, qs_ref, ks_ref, vs_ref, halo_ref, s_ref):
    tb = q_ref.shape[0]
    t = pl.program_id(0)

    @pl.when(t == 0)
    def _():
        halo_ref[...] = jnp.zeros(halo_ref.shape, F32)
        s_ref[...] = jnp.zeros(s_ref.shape, F32)

    for a, (src, dst) in enumerate(((q_ref, qs_ref), (k_ref, ks_ref), (v_ref, vs_ref))):
        for h in range(N_HEADS):
            cs = slice(h * HEAD_DIM, (h + 1) * HEAD_DIM)
            raw = src[:, cs]
            ext = jnp.concatenate([halo_ref[a, :, cs], raw], axis=0)
            cw = cw_ref[:, a * DN_W + h * HEAD_DIM:a * DN_W + (h + 1) * HEAD_DIM]
            c = (ext[5:5 + tb] * cw[0:1] + ext[6:6 + tb] * cw[1:2]
                 + ext[7:7 + tb] * cw[2:3] + raw * cw[3:4])
            c = c * _sigmoid(c)
            if a < 2:
                c = c * lax.rsqrt(jnp.sum(c * c, axis=-1, keepdims=True) + EPS)
            if a == 0:
                c = c * (HEAD_DIM ** -0.5)
            dst[:, cs] = c
            halo_ref[a, :, cs] = raw[tb - 8:tb]

    ii = lax.broadcasted_iota(I32, (CHUNK, CHUNK), 0)
    jj = lax.broadcasted_iota(I32, (CHUNK, CHUNK), 1)
    eye = (ii == jj).astype(F32)
    nw = nw_ref[...]

    heads = range(N_HEADS)
    cols = [slice(h * HEAD_DIM, (h + 1) * HEAD_DIM) for h in heads]

    def chunk(c, carry):
        rows = pl.ds(pl.multiple_of(c * CHUNK, CHUNK), CHUNK)
        cg = t * (tb // CHUNK) + c
        gcb = gc_ref[rows, :]
        btb = beta_ref[rows, :]
        q = [qs_ref[rows, cs] for cs in cols]
        k = [ks_ref[rows, cs] for cs in cols]
        v = [vs_ref[rows, cs] for cs in cols]
        z = [z_ref[rows, cs] for cs in cols]
        s = [s_ref[h] for h in heads]
        gcol = [gcb[:, h:h + 1] for h in heads]
        bcol = [btb[:, h:h + 1] for h in heads]
        grow = [gct_ref[h, pl.ds(cg, 1), :] for h in heads]
        gl = [g[:, CHUNK - 1:CHUNK] for g in grow]
        eg = [jnp.exp(g) for g in gcol]
        kb = [k[h] * bcol[h] for h in heads]
        a1 = [_mm_nt(jnp.concatenate([kb[h], q[h]], axis=0), k[h]) for h in heads]
        dec = [jnp.where(ii >= jj, jnp.exp(jnp.minimum(gcol[h] - grow[h], 0.0)), 0.0) for h in heads]
        lmat = [jnp.where(ii > jj, a1[h][:CHUNK] * dec[h], 0.0) for h in heads]
        qkd = [a1[h][CHUNK:] * dec[h] for h in heads]
        p = [eye - lmat[h] for h in heads]
        cur = [_mm(lmat[h], lmat[h]) for h in heads]
        power = 2
        while 2 * power < CHUNK:
            r = [_mm(jnp.concatenate([p[h], cur[h]], axis=0), cur[h]) for h in heads]
            p = [p[h] + r[h][:CHUNK] for h in heads]
            cur = [r[h][CHUNK:] for h in heads]
            power *= 2
        p = [p[h] + _mm(p[h], cur[h]) for h in heads]
        uw = [_mm(p[h], jnp.concatenate([v[h] * bcol[h], kb[h] * eg[h]], axis=1)) for h in heads]
        ws = [_mm(jnp.concatenate([uw[h][:, HEAD_DIM:], q[h] * eg[h]], axis=0), s[h]) for h in heads]
        vn = [uw[h][:, :HEAD_DIM] - ws[h][:CHUNK] for h in heads]
        o = [ws[h][CHUNK:] + _mm(qkd[h], vn[h]) for h in heads]
        kt = [k[h] * jnp.exp(gl[h] - gcol[h]) for h in heads]
        s_new = [s[h] * jnp.exp(gl[h]) + _mm_tn(kt[h], vn[h]) for h in heads]
        for h in heads:
            on = o[h] * lax.rsqrt(jnp.mean(o[h] * o[h], axis=-1, keepdims=True) + EPS) * nw
            y_ref[rows, cols[h]] = (on * (z[h] * _sigmoid(z[h]))).astype(BF16)
            s_ref[h] = s_new[h]
        return carry

    lax.fori_loop(0, tb // CHUNK, chunk, 0)


def _gdn(proj, gc, beta, gct, conv_w, norm_w, tb=256):
    T = proj.shape[0]
    qb = (3 * CONV_W) // DN_W
    n_chunks = T // CHUNK
    return pl.pallas_call(
        _gdn_kernel,
        grid=(T // tb,),
        in_specs=[pl.BlockSpec((tb, DN_W), lambda t: (t, qb)),
                  pl.BlockSpec((tb, DN_W), lambda t: (t, qb + 1)),
                  pl.BlockSpec((tb, DN_W), lambda t: (t, qb + 2)),
                  pl.BlockSpec((tb, DN_W), lambda t: (t, qb + 3)),
                  pl.BlockSpec((tb, N_HEADS), lambda t: (t, 0)),
                  pl.BlockSpec((tb, N_HEADS), lambda t: (t, 0)),
                  pl.BlockSpec((N_HEADS, n_chunks, CHUNK), lambda t: (0, 0, 0)),
                  pl.BlockSpec((4, 3 * DN_W), lambda t: (0, 0)),
                  pl.BlockSpec((1, HEAD_DIM), lambda t: (0, 0))],
        out_specs=pl.BlockSpec((tb, DN_W), lambda t: (t, 0)),
        out_shape=jax.ShapeDtypeStruct((T, DN_W), BF16),
        scratch_shapes=[pltpu.VMEM((tb, DN_W), F32),
                        pltpu.VMEM((tb, DN_W), F32),
                        pltpu.VMEM((tb, DN_W), F32),
                        pltpu.VMEM((3, 8, DN_W), F32),
                        pltpu.VMEM((N_HEADS, HEAD_DIM, HEAD_DIM), F32)],
        compiler_params=_cparams(("arbitrary",)),
        name="gdn",
    )(proj, proj, proj, proj, gc, beta, gct, conv_w, norm_w)


def _out_proj_kernel(ya_ref, yb_ref, w_ref, x_ref, o_ref):
    o_ref[...] = (x_ref[...]
                  + jnp.dot(ya_ref[...], w_ref[:CONV_W].astype(BF16), preferred_element_type=F32)
                  + jnp.dot(yb_ref[...], w_ref[CONV_W:].astype(BF16), preferred_element_type=F32))


def _out_proj(ya, yb, w_out, x, bm=512, bn=1024):
    T, D = x.shape
    K = w_out.shape[0]
    return pl.pallas_call(
        _out_proj_kernel,
        grid=(D // bn, T // bm),
        in_specs=[pl.BlockSpec((bm, CONV_W), lambda j, i: (i, 0)),
                  pl.BlockSpec((bm, DN_W), lambda j, i: (i, 0)),
                  pl.BlockSpec((K, bn), lambda j, i: (0, j)),
                  pl.BlockSpec((bm, bn), lambda j, i: (i, j))],
        out_specs=pl.BlockSpec((bm, bn), lambda j, i: (i, j)),
        out_shape=jax.ShapeDtypeStruct((T, D), F32),
        compiler_params=_cparams(("arbitrary", "arbitrary")),
        name="out_proj",
    )(ya, yb, w_out, x)


def _router_kernel(h_ref, nw_ref, wr_ref, br_ref, xp_ref, idx_ref, gate_ref, rank_ref, cnt_ref,
                   carry_ref):
    bm, D = h_ref.shape

    @pl.when(pl.program_id(0) == 0)
    def _():
        carry_ref[...] = jnp.zeros(carry_ref.shape, F32)

    h = h_ref[...]
    ms = jnp.mean(h * h, axis=-1, keepdims=True)
    xn = h * lax.rsqrt(ms + EPS) * nw_ref[...]
    xp_ref[...] = pltpu.pack_elementwise([xn[:, :D // 2], xn[:, D // 2:]], packed_dtype=BF16)

    logits = _mm3_nt(xn, wr_ref[...]) + br_ref[...]
    lane_e = lax.broadcasted_iota(I32, logits.shape, 1)
    work = logits
    sels, vals, idxs = [], [], []
    for _ in range(TOP_K):
        m = jnp.max(work, axis=-1, keepdims=True)
        idx = jnp.min(jnp.where(work == m, lane_e, N_EXPERTS), axis=-1, keepdims=True)
        sel = lane_e == idx
        work = jnp.where(sel, -jnp.inf, work)
        sels.append(sel)
        vals.append(m)
        idxs.append(idx)
    exps = [jnp.exp(v - vals[0]) for v in vals]
    denom = exps[0] + exps[1] + exps[2] + exps[3]
    gates = [e / denom for e in exps]

    onehot = jnp.zeros(logits.shape, F32)
    for sel in sels:
        onehot = onehot + sel.astype(F32)
    ri = lax.broadcasted_iota(I32, (bm, bm), 0)
    ci = lax.broadcasted_iota(I32, (bm, bm), 1)
    tri = (ri > ci).astype(BF16)
    carry = carry_ref[...]
    before = jnp.dot(tri, onehot.astype(BF16), preferred_element_type=F32) + carry
    ranks = [jnp.sum(jnp.where(sel, before, 0.0), axis=-1, keepdims=True).astype(I32)
             for sel in sels]
    new_carry = carry + jnp.sum(onehot, axis=0, keepdims=True)
    carry_ref[...] = new_carry
    cnt_ref[...] = new_carry

    lane = lax.broadcasted_iota(I32, (bm, LANE), 1)

    def spread(cols):
        return jnp.where(lane == 0, cols[0],
                         jnp.where(lane == 1, cols[1], jnp.where(lane == 2, cols[2], cols[3])))

    idx_ref[...] = spread(idxs)
    gate_ref[...] = spread(gates)
    rank_ref[...] = spread(ranks)


def _router(h1, nw, w_router_t, b_router, bm=256):
    T, D = h1.shape
    return pl.pallas_call(
        _router_kernel,
        grid=(T // bm,),
        in_specs=[pl.BlockSpec((bm, D), lambda i: (i, 0)),
                  pl.BlockSpec((1, D), lambda i: (0, 0)),
                  pl.BlockSpec((N_EXPERTS, D), lambda i: (0, 0)),
                  pl.BlockSpec((1, N_EXPERTS), lambda i: (0, 0))],
        out_specs=[pl.BlockSpec((bm, D // 2), lambda i: (i, 0)),
                   pl.BlockSpec((bm, LANE), lambda i: (i, 0)),
                   pl.BlockSpec((bm, LANE), lambda i: (i, 0)),
                   pl.BlockSpec((bm, LANE), lambda i: (i, 0)),
                   pl.BlockSpec((1, N_EXPERTS), lambda i: (0, 0))],
        out_shape=[jax.ShapeDtypeStruct((T, D // 2), jnp.uint32),
                   jax.ShapeDtypeStruct((T, LANE), I32),
                   jax.ShapeDtypeStruct((T, LANE), F32),
                   jax.ShapeDtypeStruct((T, LANE), I32),
                   jax.ShapeDtypeStruct((1, N_EXPERTS), F32)],
        scratch_shapes=[pltpu.VMEM((1, N_EXPERTS), F32)],
        compiler_params=_cparams(("arbitrary",)),
        name="router",
    )(h1, nw, w_router_t, b_router)


def _dispatch_kernel(dest_ref, pad_start_ref, pad_len_ref, nused_ref, x_ref, xs_ref, zero_ref, sem):
    tb = x_ref.shape[0]
    i = pl.program_id(0)
    n_blocks = xs_ref.shape[0] // SLOT_BLK

    def row_copy(r, slot):
        return pltpu.make_async_copy(x_ref.at[pl.ds(r, 1)], xs_ref.at[pl.ds(slot, 1)], sem)

    def zero_row_copy(slot):
        return pltpu.make_async_copy(zero_ref.at[pl.ds(0, 1)], xs_ref.at[pl.ds(slot, 1)], sem)

    def zero_blk_copy(b):
        start = pl.multiple_of(b * SLOT_BLK, SLOT_BLK)
        return pltpu.make_async_copy(zero_ref, xs_ref.at[pl.ds(start, SLOT_BLK)], sem)

    @pl.when(i == 0)
    def _():
        zero_ref[...] = jnp.zeros(zero_ref.shape, zero_ref.dtype)

        def per_expert(e, c):
            start = pad_start_ref[e]
            n = pad_len_ref[e]
            lax.fori_loop(0, n, lambda j, cc: (zero_row_copy(start + j).start(), cc)[1], 0)
            lax.fori_loop(0, n, lambda j, cc: (zero_row_copy(start + j).wait(), cc)[1], 0)
            return c

        lax.fori_loop(0, N_EXPERTS, per_expert, 0)
        nu = nused_ref[0]
        lax.fori_loop(nu, n_blocks, lambda b, cc: (zero_blk_copy(b).start(), cc)[1], 0)
        lax.fori_loop(nu, n_blocks, lambda b, cc: (zero_blk_copy(b).wait(), cc)[1], 0)

    def issue(r, c):
        base = (i * tb + r) * TOP_K
        for k in range(TOP_K):
            row_copy(r, dest_ref[base + k]).start()
        return c

    def drain(r, c):
        for k in range(TOP_K):
            row_copy(0, 0).wait()
        return c

    lax.fori_loop(0, tb, issue, 0, unroll=DMA_ISSUE_UNROLL)
    lax.fori_loop(0, tb, drain, 0, unroll=DMA_ISSUE_UNROLL)


def _dispatch(dest_flat, pad_start, pad_len, nused, xp, n_slots, tb=256):
    T, W = xp.shape
    return pl.pallas_call(
        _dispatch_kernel,
        grid_spec=pltpu.PrefetchScalarGridSpec(
            num_scalar_prefetch=4,
            grid=(T // tb,),
            in_specs=[pl.BlockSpec((tb, W), lambda i, *_: (i, 0))],
            out_specs=pl.BlockSpec(memory_space=pl.ANY),
            scratch_shapes=[pltpu.VMEM((SLOT_BLK, W), xp.dtype),
                            pltpu.SemaphoreType.DMA(())],
        ),
        out_shape=jax.ShapeDtypeStruct((n_slots, W), xp.dtype),
        compiler_params=_cparams(("arbitrary",)),
        name="dispatch",
    )(dest_flat, pad_start, pad_len, nused, xp)


def _unpack_pair(xw):
    xa = pltpu.unpack_elementwise(xw, index=0, packed_dtype=BF16, unpacked_dtype=F32)
    xb = pltpu.unpack_elementwise(xw, index=1, packed_dtype=BF16, unpacked_dtype=F32)
    return xa.astype(BF16), xb.astype(BF16)


def _moe_kernel(item_e_ref, item_start_ref, item_nsub_ref, nused_ref,
                xs_ref, wgu_ref, wd_ref, bgu_ref, bd_ref, ys_ref,
                x_ref, h_ref, g_ref, gu_ref, d_ref, stage_ref, pend_ref,
                sem_x, sem_gu, sem_d, sem_o):
    i = pl.program_id(0)
    tf = gu_ref.shape[-1]
    tn = d_ref.shape[-1]
    half = gu_ref.shape[1] // 2
    nf = EXPERT_FF // tf
    nn = wd_ref.shape[-1] // tn
    n_blocks = ys_ref.shape[0] // MOE_SUB
    tp = tn // 2
    GATE, UP = 0, 1

    e = item_e_ref[i]
    start = pl.multiple_of(item_start_ref[i], MOE_SUB)
    nsub = item_nsub_ref[i]
    nxt_e = item_e_ref[i + 1]
    nxt_start = pl.multiple_of(item_start_ref[i + 1], MOE_SUB)
    nxt_nsub = item_nsub_ref[i + 1]

    gu_rows = gu_ref.shape[1] // MOE_DMA_SPLIT
    d_rows = d_ref.shape[1] // MOE_DMA_SPLIT

    def gu_copies(ee, part, f, slot=None):
        slot = part if slot is None else slot
        col0 = part * EXPERT_FF + pl.multiple_of(f * tf, tf)
        return [pltpu.make_async_copy(
            wgu_ref.at[0, ee, pl.ds(s * gu_rows, gu_rows), pl.ds(col0, tf)],
            gu_ref.at[slot, pl.ds(s * gu_rows, gu_rows)], sem_gu.at[slot])
            for s in range(MOE_DMA_SPLIT)]

    def d_copies(ee, n, slot):
        return [pltpu.make_async_copy(
            wd_ref.at[0, ee, pl.ds(s * d_rows, d_rows), pl.ds(pl.multiple_of(n * tn, tn), tn)],
            d_ref.at[slot, pl.ds(s * d_rows, d_rows)], sem_d.at[slot])
            for s in range(MOE_DMA_SPLIT)]

    def x_copy(row0, r):
        off = pl.multiple_of(r * MOE_SUB, MOE_SUB)
        return pltpu.make_async_copy(xs_ref.at[pl.ds(row0 + off, MOE_SUB)],
                                     x_ref.at[pl.ds(off, MOE_SUB)], sem_x)

    def out_copy(row0, nrows, n, slot):
        return pltpu.make_async_copy(
            stage_ref.at[slot, pl.ds(0, nrows)],
            ys_ref.at[pl.ds(row0, nrows), pl.ds(pl.multiple_of(n * tp, tp), tp)], sem_o.at[slot])

    def wait_out(slot):
        for units in (1, 2, MOE_BLOCK_UNITS):
            @pl.when(pend_ref[slot] == units)
            def _(units=units):
                out_copy(start, units * MOE_SUB, 0, slot).wait()

    def for_row_blocks(body):
        nfull = nsub // MOE_BLOCK_UNITS
        rem = nsub % MOE_BLOCK_UNITS
        full_rows = MOE_BLOCK_UNITS * MOE_SUB

        def full(p, c):
            body(p, pl.multiple_of(p * full_rows, full_rows), full_rows)
            return c

        lax.fori_loop(0, nfull, full, 0)
        has2 = (rem // 2) % 2

        @pl.when(has2 == 1)
        def _():
            body(nfull, pl.multiple_of(nfull * full_rows, full_rows), 2 * MOE_SUB)

        @pl.when(rem % 2 == 1)
        def _():
            body(nfull + has2,
                 pl.multiple_of(nfull * full_rows + has2 * 2 * MOE_SUB, MOE_SUB), MOE_SUB)

    def start_gu(ee, part, f):
        for cp in gu_copies(ee, part, f):
            cp.start()

    def wait_gu(part):
        for cp in gu_copies(e, part, 0):
            cp.wait()

    def start_d(ee, n, slot):
        for cp in d_copies(ee, n, slot):
            cp.start()

    def load_rows(row0, count):
        lax.fori_loop(0, count, lambda r, c: (x_copy(row0, r).start(), c)[1], 0)

    @pl.when(i == 0)
    def _():
        pend_ref[0] = 0
        pend_ref[1] = 0
        stage_ref[0] = jnp.zeros(stage_ref.shape[1:], stage_ref.dtype)

        def zero_block(b, c):
            row0 = pl.multiple_of(b * MOE_SUB, MOE_SUB)
            for n in range(nn):
                out_copy(row0, MOE_SUB, n, 0).start()
            for n in range(nn):
                out_copy(row0, MOE_SUB, n, 0).wait()
            return c

        lax.fori_loop(nused_ref[0], n_blocks, zero_block, 0)

        @pl.when(nsub > 0)
        def _():
            load_rows(start, nsub)
            start_gu(e, GATE, 0)

    def x_dot(rows, slot):
        xa, xb = _unpack_pair(x_ref[rows, :])
        return (jnp.dot(xa, gu_ref[slot, :half].astype(BF16), preferred_element_type=F32)
                + jnp.dot(xb, gu_ref[slot, half:].astype(BF16), preferred_element_type=F32))

    def gate_up_tiles(f, c):
        wait_gu(GATE)
        start_gu(e, UP, f)
        bias_g = bgu_ref[e, pl.ds(f, 1), :]

        def gate_rows(blk, off, nrows):
            rows = pl.ds(off, nrows)
            g_ref[rows, :] = jnp.minimum(x_dot(rows, GATE) + bias_g, SWIGLU_LIMIT)

        for_row_blocks(gate_rows)

        wait_gu(UP)

        @pl.when(f + 1 < nf)
        def _():
            start_gu(e, GATE, f + 1)

        @pl.when(f + 1 == nf)
        def _():
            start_d(e, 0, 0)

        @pl.when(jnp.logical_and(f + 1 == nf, nxt_nsub > 0))
        def _():
            start_gu(nxt_e, GATE, 0)

        bias_u = bgu_ref[e, pl.ds(nf + f, 1), :]

        def up_rows(blk, off, nrows):
            rows = pl.ds(off, nrows)
            gate = g_ref[rows, :]
            up = jnp.clip(x_dot(rows, UP) + bias_u, -SWIGLU_LIMIT, SWIGLU_LIMIT)
            glu = gate * _sigmoid(SWIGLU_ALPHA * gate)
            h_ref[f, rows, :] = ((up + 1.0) * glu).astype(BF16)

        for_row_blocks(up_rows)
        return c

    def down_tiles(n, c):
        slot = n % 2
        for cp in d_copies(e, n, slot):
            cp.wait()

        @pl.when(n + 1 < nn)
        def _():
            start_d(e, n + 1, 1 - slot)

        @pl.when(jnp.logical_and(n == 0, nxt_nsub > 0))
        def _():
            load_rows(nxt_start, nxt_nsub)

        bias_d = bd_ref[e, pl.ds(n, 1), :]

        def down_rows(blk, off, nrows):
            os = blk % 2
            wait_out(os)
            rows = pl.ds(off, nrows)
            acc = bias_d
            for f in range(nf):
                acc = acc + jnp.dot(h_ref[f, rows, :], d_ref[slot, f * tf:(f + 1) * tf, :].astype(BF16),
                                    preferred_element_type=F32)
            stage_ref[os, pl.ds(0, nrows)] = pltpu.pack_elementwise(
                [acc[:, :tp], acc[:, tp:]], packed_dtype=BF16)
            out_copy(start + off, nrows, n, os).start()
            pend_ref[os] = nrows // MOE_SUB

        for_row_blocks(down_rows)
        return c

    @pl.when(nsub > 0)
    def _():
        lax.fori_loop(0, nsub, lambda r, c: (x_copy(start, r).wait(), c)[1], 0)
        lax.fori_loop(0, nf, gate_up_tiles, 0)
        lax.fori_loop(0, nn, down_tiles, 0)
        for os in range(2):
            wait_out(os)
            pend_ref[os] = 0


def _moe(item_e, item_start, item_nsub, nused, xs, w_gate_up, b_gate_up, w_down, b_down,
         tf=MOE_TF, tn=MOE_TN):
    n_slots, W = xs.shape
    D = 2 * W
    n_items = item_e.shape[0] - 1
    rb = MOE_ITEM_SUBS * MOE_SUB
    b_gate_up = b_gate_up.reshape(N_EXPERTS, 2 * EXPERT_FF // tf, tf)
    b_down = b_down.reshape(N_EXPERTS, D // tn, tn)
    any_spec = pl.BlockSpec(memory_space=pl.ANY)
    return pl.pallas_call(
        _moe_kernel,
        grid_spec=pltpu.PrefetchScalarGridSpec(
            num_scalar_prefetch=4,
            grid=(n_items,),
            in_specs=[any_spec, any_spec, any_spec,
                      pl.BlockSpec(b_gate_up.shape, lambda i, *_: (0, 0, 0)),
                      pl.BlockSpec(b_down.shape, lambda i, *_: (0, 0, 0))],
            out_specs=any_spec,
            scratch_shapes=[pltpu.VMEM((rb, W), xs.dtype),
                            pltpu.VMEM((EXPERT_FF // tf, rb, tf), BF16),
                            pltpu.VMEM((rb, tf), F32),
                            pltpu.VMEM((2, D, tf), F32),
                            pltpu.VMEM((2, EXPERT_FF, tn), F32),
                            pltpu.VMEM((2, MOE_BLOCK_UNITS * MOE_SUB, tn // 2), xs.dtype),
                            pltpu.SMEM((2,), I32),
                            pltpu.SemaphoreType.DMA(()),
                            pltpu.SemaphoreType.DMA((2,)),
                            pltpu.SemaphoreType.DMA((2,)),
                            pltpu.SemaphoreType.DMA((2,))],
        ),
        out_shape=jax.ShapeDtypeStruct((n_slots, D // 2), xs.dtype),
        compiler_params=_cparams(("arbitrary",)),
        name="moe",
    )(item_e, item_start, item_nsub, nused, xs, w_gate_up, w_down, b_gate_up, b_down)


def _combine_kernel(dest_ref, h_ref, gate_ref, ys_ref, nw_ref, o_ref, buf_ref, sem):
    tb = h_ref.shape[0]
    i = pl.program_id(0)
    n_steps = pl.num_programs(0)

    def row_copy(slot, buf, k, r):
        return pltpu.make_async_copy(ys_ref.at[pl.ds(slot, 1)], buf_ref.at[buf, k, pl.ds(r, 1)],
                                     sem.at[buf])

    def gather(step, buf):
        def issue(r, c):
            base = (step * tb + r) * TOP_K
            for k in range(TOP_K):
                row_copy(dest_ref[base + k], buf, k, r).start()
            return c

        lax.fori_loop(0, tb, issue, 0, unroll=DMA_ISSUE_UNROLL)

    @pl.when(i == 0)
    def _():
        gather(0, 0)

    cur = i % 2

    @pl.when(i + 1 < n_steps)
    def _():
        gather(i + 1, 1 - cur)

    def drain(r, c):
        for k in range(TOP_K):
            row_copy(0, cur, k, r).wait()
        return c

    lax.fori_loop(0, tb, drain, 0, unroll=DMA_ISSUE_UNROLL)

    gates = gate_ref[...]
    D = h_ref.shape[1]
    tp = MOE_TN // 2
    ssq = jnp.zeros((tb, 1), F32)
    for n in range(D // MOE_TN):
        cols = (slice(n * MOE_TN, n * MOE_TN + tp), slice(n * MOE_TN + tp, (n + 1) * MOE_TN))
        parts = [h_ref[:, c] for c in cols]
        for k in range(TOP_K):
            words = buf_ref[cur, k, :, n * tp:(n + 1) * tp]
            g = gates[:, k:k + 1]
            for half in range(2):
                y = pltpu.unpack_elementwise(words, index=half, packed_dtype=BF16,
                                             unpacked_dtype=F32)
                parts[half] = parts[half] + y * g
        for half in range(2):
            o_ref[:, cols[half]] = parts[half]
            ssq = ssq + jnp.sum(parts[half] * parts[half], axis=-1, keepdims=True)
    o_ref[...] = o_ref[...] * lax.rsqrt(ssq / D + EPS) * nw_ref[...]


def _combine(dest_flat, h1, gates, ys, nw, tb=128):
    T, D = h1.shape
    return pl.pallas_call(
        _combine_kernel,
        grid_spec=pltpu.PrefetchScalarGridSpec(
            num_scalar_prefetch=1,
            grid=(T // tb,),
            in_specs=[pl.BlockSpec((tb, D), lambda i, *_: (i, 0)),
                      pl.BlockSpec((tb, LANE), lambda i, *_: (i, 0)),
                      pl.BlockSpec(memory_space=pl.ANY),
                      pl.BlockSpec((1, D), lambda i, *_: (0, 0))],
            out_specs=pl.BlockSpec((tb, D), lambda i, *_: (i, 0)),
            scratch_shapes=[pltpu.VMEM((2, TOP_K, tb, D // 2), ys.dtype),
                            pltpu.SemaphoreType.DMA((2,))],
        ),
        out_shape=jax.ShapeDtypeStruct((T, D), F32),
        compiler_params=_cparams(("arbitrary",)),
        name="combine",
    )(dest_flat, h1, gates, ys, nw)


def _layer(h, norm1_w, w_in, conv_a_w, conv_a_norm_w, dn_conv_w, dn_A_log, dn_dt_bias, dn_norm_w,
           w_out, norm2_w, w_router, b_router, w_gate_up, b_gate_up, w_down, b_down, l):
    T, D = h.shape
    main_cols = 3 * CONV_W + 4 * DN_W
    w_in_t = w_in[l].T
    xn, gc, beta = _norm1(h, norm1_w[l][None], w_in_t, main_cols, dn_A_log[l][None],
                          dn_dt_bias[l][None])
    proj = _in_proj(xn, w_in_t, main_cols)
    ya = _conv_mix(proj, conv_a_w[l], conv_a_norm_w[l][None])
    gct = gc.T.reshape(N_HEADS, T // CHUNK, CHUNK)
    yb = _gdn(proj, gc, beta, gct, dn_conv_w[l], dn_norm_w[l][None])
    h1 = _out_proj(ya, yb, w_out[l], h)

    xp, idx, gates, rank, counts = _router(h1, norm2_w[l][None], w_router[l].T, b_router[l][None])
    top_idx = idx[:, :TOP_K]
    counts = counts[0].astype(I32)
    padded = ((counts + SLOT_BLK - 1) // SLOT_BLK) * SLOT_BLK
    pend = jnp.cumsum(padded)
    pstart = pend - padded
    dest_flat = (pstart[top_idx] + rank[:, :TOP_K]).reshape(-1)
    n_slots = T * TOP_K + N_EXPERTS * SLOT_BLK
    nb = n_slots // SLOT_BLK
    nused = (pend[-1] // SLOT_BLK).reshape(1)

    nsub_e = padded // SLOT_BLK
    items_e = (nsub_e + MOE_ITEM_SUBS - 1) // MOE_ITEM_SUBS
    item_end = jnp.cumsum(items_e)
    item_first = item_end - items_e
    n_items = N_EXPERTS + nb // MOE_ITEM_SUBS
    item = jnp.arange(n_items + 1, dtype=I32)
    item_e = jnp.minimum(jnp.sum((item_end[None, :] <= item[:, None]).astype(I32), axis=1),
                         N_EXPERTS - 1)
    item_j = item - item_first[item_e]
    item_valid = item < item_end[-1]
    item_start = jnp.where(item_valid, pstart[item_e] + item_j * (MOE_ITEM_SUBS * SLOT_BLK), 0)
    item_nsub = jnp.where(item_valid,
                          jnp.minimum(nsub_e[item_e] - item_j * MOE_ITEM_SUBS, MOE_ITEM_SUBS), 0)

    xs = _dispatch(dest_flat, pstart + counts, padded - counts, nused, xp, n_slots)
    ys = _moe(item_e, item_start.astype(I32), item_nsub.astype(I32), nused, xs,
              w_gate_up[l:l + 1], b_gate_up[l], w_down[l:l + 1], b_down[l])
    return h1, dest_flat, gates, ys


def kernel(x, norm1_w, w_in, conv_a_w, conv_a_norm_w, dn_conv_w, dn_A_log, dn_dt_bias, dn_norm_w,
           w_out, norm2_w, w_router, b_router, w_gate_up, b_gate_up, w_down, b_down, final_norm_w):
    B, S, D = x.shape
    depth = norm1_w.shape[0]
    assert depth == 1, "the final norm is fused into the last layer's combine"
    h = x.reshape(B * S, D)
    h1, dest_flat, gates, ys = _layer(
        h, norm1_w, w_in, conv_a_w, conv_a_norm_w, dn_conv_w, dn_A_log, dn_dt_bias, dn_norm_w,
        w_out, norm2_w, w_router, b_router, w_gate_up, b_gate_up, w_down, b_down, 0)
    out = _combine(dest_flat, h1, gates, ys, final_norm_w[None])
    return out.reshape(B, S, D)
```

```python
import functools

import jax
import jax.numpy as jnp
from jax import lax
from jax.experimental import pallas as pl
from jax.experimental.pallas import tpu as pltpu

F32 = jnp.float32
BF16 = jnp.bfloat16
I32 = jnp.int32

EPS = 1e-6
CHUNK = 64
HEAD_DIM = 128
N_HEADS = 16
CONV_W = 2048
DN_W = 2048
N_EXPERTS = 32
TOP_K = 4
EXPERT_FF = 2048
SWIGLU_LIMIT = 7.0
SWIGLU_ALPHA = 1.702

LANE = 128
VMEM_LIMIT = 56 * 1024 * 1024
SLOT_BLK = 128
MOE_SUB = SLOT_BLK
MOE_BLOCK_UNITS = 8
MOE_ITEM_SUBS = 10
MOE_TF = 512
MOE_TN = 512
MOE_DMA_SPLIT = 4
DMA_ISSUE_UNROLL = 4


def _cparams(sem):
    return pltpu.CompilerParams(dimension_semantics=sem, vmem_limit_bytes=VMEM_LIMIT)


def _mm(a, b):
    return jnp.dot(a.astype(BF16), b.astype(BF16), preferred_element_type=F32)


def _dot_nt(a, b):
    return lax.dot_general(a, b, (((1,), (1,)), ((), ())), preferred_element_type=F32)


def _mm_nt(a, b):
    return _dot_nt(a.astype(BF16), b.astype(BF16))


def _mm_tn(a, b):
    return lax.dot_general(a.astype(BF16), b.astype(BF16), (((0,), (0,)), ((), ())),
                           preferred_element_type=F32)


def _mm3_nt(x, wt):
    xh = x.astype(BF16)
    xl = (x - xh.astype(F32)).astype(BF16)
    wh = wt.astype(BF16)
    wl = (wt - wh.astype(F32)).astype(BF16)
    return _dot_nt(xh, wh) + _dot_nt(xl, wh) + _dot_nt(xh, wl)


def _sigmoid(x):
    return 1.0 / (1.0 + jnp.exp(-x))


def _norm1_kernel(x_ref, nw_ref, wab_ref, alog_ref, dtb_ref, xn_ref, gc_ref, beta_ref):
    x = x_ref[...]
    ms = jnp.mean(x * x, axis=-1, keepdims=True)
    xn = x * lax.rsqrt(ms + EPS) * nw_ref[...]
    xn_ref[...] = xn.astype(BF16)
    ab = _mm3_nt(xn, wab_ref[...])
    a = ab[:, :N_HEADS] + dtb_ref[...]
    b = ab[:, N_HEADS:]
    softplus = jnp.maximum(a, 0.0) + jnp.log1p(jnp.exp(-jnp.abs(a)))
    g = -jnp.exp(alog_ref[...]) * softplus
    pos = lax.broadcasted_iota(I32, g.shape, 0) % CHUNK
    shift = 1
    while shift < CHUNK:
        g = g + jnp.where(pos >= shift, pltpu.roll(g, shift, 0), 0.0)
        shift *= 2
    gc_ref[...] = g
    beta_ref[...] = _sigmoid(b)


def _norm1(x, nw, w_in_t, ab_row, alog, dtb, bm=256):
    T, D = x.shape
    nab = 2 * N_HEADS
    return pl.pallas_call(
        _norm1_kernel,
        grid=(T // bm,),
        in_specs=[pl.BlockSpec((bm, D), lambda i: (i, 0)),
                  pl.BlockSpec((1, D), lambda i: (0, 0)),
                  pl.BlockSpec((nab, D), lambda i: (ab_row // nab, 0)),
                  pl.BlockSpec((1, N_HEADS), lambda i: (0, 0)),
                  pl.BlockSpec((1, N_HEADS), lambda i: (0, 0))],
        out_specs=[pl.BlockSpec((bm, D), lambda i: (i, 0)),
                   pl.BlockSpec((bm, N_HEADS), lambda i: (i, 0)),
                   pl.BlockSpec((bm, N_HEADS), lambda i: (i, 0))],
        out_shape=[jax.ShapeDtypeStruct((T, D), BF16),
                   jax.ShapeDtypeStruct((T, N_HEADS), F32),
                   jax.ShapeDtypeStruct((T, N_HEADS), F32)],
        compiler_params=_cparams(("arbitrary",)),
        name="norm1",
    )(x, nw, w_in_t, alog, dtb)


def _in_proj_kernel(x_ref, w_ref, o_ref):
    o_ref[...] = _dot_nt(x_ref[...], w_ref[...].astype(BF16))


def _in_proj(xn, w_in_t, n_cols, bm=512, bn=1024):
    T, D = xn.shape
    return pl.pallas_call(
        _in_proj_kernel,
        grid=(n_cols // bn, T // bm),
        in_specs=[pl.BlockSpec((bm, D), lambda j, i: (i, 0)),
                  pl.BlockSpec((bn, D), lambda j, i: (j, 0))],
        out_specs=pl.BlockSpec((bm, bn), lambda j, i: (i, j)),
        out_shape=jax.ShapeDtypeStruct((T, n_cols), F32),
        compiler_params=_cparams(("arbitrary", "arbitrary")),
        name="in_proj",
    )(xn, w_in_t)


def _conv_mix_kernel(x_ref, b_ref, c_ref, xh_ref, ch_ref, cw_ref, nw_ref, o_ref):
    bm, bc = x_ref.shape
    u = c_ref[...] * x_ref[...]
    halo = jnp.where(pl.program_id(0) > 0, ch_ref[...] * xh_ref[...], 0.0)
    ext = jnp.concatenate([halo, u], axis=0)
    cw = cw_ref[...]
    conv = ext[6:6 + bm] * cw[0:1] + ext[7:7 + bm] * cw[1:2] + u * cw[2:3]
    y = b_ref[...] * conv
    nw = nw_ref[...]
    for g in range(bc // LANE):
        cs = slice(g * LANE, (g + 1) * LANE)
        yg = y[:, cs]
        ms = jnp.mean(yg * yg, axis=-1, keepdims=True)
        o_ref[:, cs] = (yg * lax.rsqrt(ms + EPS) * nw[:, cs]).astype(BF16)


def _conv_mix(proj, conv_w, norm_w, bm=512, bc=512):
    T = proj.shape[0]
    nc = CONV_W // bc
    hb = bm // 8
    halo_map = lambda off: (lambda i, j: (jnp.maximum(i * hb - 1, 0), j + off))
    return pl.pallas_call(
        _conv_mix_kernel,
        grid=(T // bm, nc),
        in_specs=[pl.BlockSpec((bm, bc), lambda i, j: (i, j)),
                  pl.BlockSpec((bm, bc), lambda i, j: (i, j + nc)),
                  pl.BlockSpec((bm, bc), lambda i, j: (i, j + 2 * nc)),
                  pl.BlockSpec((8, bc), halo_map(0)),
                  pl.BlockSpec((8, bc), halo_map(2 * nc)),
                  pl.BlockSpec((3, bc), lambda i, j: (0, j)),
                  pl.BlockSpec((1, bc), lambda i, j: (0, j))],
        out_specs=pl.BlockSpec((bm, bc), lambda i, j: (i, j)),
        out_shape=jax.ShapeDtypeStruct((T, CONV_W), BF16),
        compiler_params=_cparams(("arbitrary", "arbitrary")),
        name="conv_mix",
    )(proj, proj, proj, proj, proj, conv_w, norm_w)


def _gdn_kernel(q_ref, k_ref, v_ref, z_ref, gc_ref, beta_ref, gct_ref, cw_ref, nw_ref,
                y_ref, qs_ref, ks_ref, vs_ref, halo_ref, s_ref):
    tb = q_ref.shape[0]
    t = pl.program_id(0)

    @pl.when(t == 0)
    def _():
        halo_ref[...] = jnp.zeros(halo_ref.shape, F32)
        s_ref[...] = jnp.zeros(s_ref.shape, F32)

    for a, (src, dst) in enumerate(((q_ref, qs_ref), (k_ref, ks_ref), (v_ref, vs_ref))):
        for h in range(N_HEADS):
            cs = slice(h * HEAD_DIM, (h + 1) * HEAD_DIM)
            raw = src[:, cs]
            ext = jnp.concatenate([halo_ref[a, :, cs], raw], axis=0)
            cw = cw_ref[:, a * DN_W + h * HEAD_DIM:a * DN_W + (h + 1) * HEAD_DIM]
            c = (ext[5:5 + tb] * cw[0:1] + ext[6:6 + tb] * cw[1:2]
                 + ext[7:7 + tb] * cw[2:3] + raw * cw[3:4])
            c = c * _sigmoid(c)
            if a < 2:
                c = c * lax.rsqrt(jnp.sum(c * c, axis=-1, keepdims=True) + EPS)
            if a == 0:
                c = c * (HEAD_DIM ** -0.5)
            dst[:, cs] = c
            halo_ref[a, :, cs] = raw[tb - 8:tb]

    ii = lax.broadcasted_iota(I32, (CHUNK, CHUNK), 0)
    jj = lax.broadcasted_iota(I32, (CHUNK, CHUNK), 1)
    eye = (ii == jj).astype(F32)
    nw = nw_ref[...]

    heads = range(N_HEADS)
    cols = [slice(h * HEAD_DIM, (h + 1) * HEAD_DIM) for h in heads]

    def chunk(c, carry):
        rows = pl.ds(pl.multiple_of(c * CHUNK, CHUNK), CHUNK)
        cg = t * (tb // CHUNK) + c
        gcb = gc_ref[rows, :]
        btb = beta_ref[rows, :]
        q = [qs_ref[rows, cs] for cs in cols]
        k = [ks_ref[rows, cs] for cs in cols]
        v = [vs_ref[rows, cs] for cs in cols]
        z = [z_ref[rows, cs] for cs in cols]
        s = [s_ref[h] for h in heads]
        gcol = [gcb[:, h:h + 1] for h in heads]
        bcol = [btb[:, h:h + 1] for h in heads]
        grow = [gct_ref[h, pl.ds(cg, 1), :] for h in heads]
        gl = [g[:, CHUNK - 1:CHUNK] for g in grow]
        eg = [jnp.exp(g) for g in gcol]
        kb = [k[h] * bcol[h] for h in heads]
        a1 = [_mm_nt(jnp.concatenate([kb[h], q[h]], axis=0), k[h]) for h in heads]
        dec = [jnp.where(ii >= jj, jnp.exp(jnp.minimum(gcol[h] - grow[h], 0.0)), 0.0) for h in heads]
        lmat = [jnp.where(ii > jj, a1[h][:CHUNK] * dec[h], 0.0) for h in heads]
        qkd = [a1[h][CHUNK:] * dec[h] for h in heads]
        p = [eye - lmat[h] for h in heads]
        cur = [_mm(lmat[h], lmat[h]) for h in heads]
        power = 2
        while 2 * power < CHUNK:
            r = [_mm(jnp.concatenate([p[h], cur[h]], axis=0), cur[h]) for h in heads]
            p = [p[h] + r[h][:CHUNK] for h in heads]
            cur = [r[h][CHUNK:] for h in heads]
            power *= 2
        p = [p[h] + _mm(p[h], cur[h]) for h in heads]
        uw = [_mm(p[h], jnp.concatenate([v[h] * bcol[h], kb[h] * eg[h]], axis=1)) for h in heads]
        ws = [_mm(jnp.concatenate([uw[h][:, HEAD_DIM:], q[h] * eg[h]], axis=0), s[h]) for h in heads]
        vn = [uw[h][:, :HEAD_DIM] - ws[h][:CHUNK] for h in heads]
        o = [ws[h][CHUNK:] + _mm(qkd[h], vn[h]) for h in heads]
        kt = [k[h] * jnp.exp(gl[h] - gcol[h]) for h in heads]
        s_new = [s[h] * jnp.exp(gl[h]) + _mm_tn(kt[h], vn[h]) for h in heads]
        for h in heads:
            on = o[h] * lax.rsqrt(jnp.mean(o[h] * o[h], axis=-1, keepdims=True) + EPS) * nw
            y_ref[rows, cols[h]] = (on * (z[h] * _sigmoid(z[h]))).astype(BF16)
            s_ref[h] = s_new[h]
        return carry

    lax.fori_loop(0, tb // CHUNK, chunk, 0)


def _gdn(proj, gc, beta, gct, conv_w, norm_w, tb=256):
    T = proj.shape[0]
    qb = (3 * CONV_W) // DN_W
    n_chunks = T // CHUNK
    return pl.pallas_call(
        _gdn_kernel,
        grid=(T // tb,),
        in_specs=[pl.BlockSpec((tb, DN_W), lambda t: (t, qb)),
                  pl.BlockSpec((tb, DN_W), lambda t: (t, qb + 1)),
                  pl.BlockSpec((tb, DN_W), lambda t: (t, qb + 2)),
                  pl.BlockSpec((tb, DN_W), lambda t: (t, qb + 3)),
                  pl.BlockSpec((tb, N_HEADS), lambda t: (t, 0)),
                  pl.BlockSpec((tb, N_HEADS), lambda t: (t, 0)),
                  pl.BlockSpec((N_HEADS, n_chunks, CHUNK), lambda t: (0, 0, 0)),
                  pl.BlockSpec((4, 3 * DN_W), lambda t: (0, 0)),
                  pl.BlockSpec((1, HEAD_DIM), lambda t: (0, 0))],
        out_specs=pl.BlockSpec((tb, DN_W), lambda t: (t, 0)),
        out_shape=jax.ShapeDtypeStruct((T, DN_W), BF16),
        scratch_shapes=[pltpu.VMEM((tb, DN_W), F32),
                        pltpu.VMEM((tb, DN_W), F32),
                        pltpu.VMEM((tb, DN_W), F32),
                        pltpu.VMEM((3, 8, DN_W), F32),
                        pltpu.VMEM((N_HEADS, HEAD_DIM, HEAD_DIM), F32)],
        compiler_params=_cparams(("arbitrary",)),
        name="gdn",
    )(proj, proj, proj, proj, gc, beta, gct, conv_w, norm_w)


def _out_proj_kernel(ya_ref, yb_ref, w_ref, x_ref, o_ref):
    o_ref[...] = (x_ref[...]
                  + jnp.dot(ya_ref[...], w_ref[:CONV_W].astype(BF16), preferred_element_type=F32)
                  + jnp.dot(yb_ref[...], w_ref[CONV_W:].astype(BF16), preferred_element_type=F32))


def _out_proj(ya, yb, w_out, x, bm=512, bn=1024):
    T, D = x.shape
    K = w_out.shape[0]
    return pl.pallas_call(
        _out_proj_kernel,
        grid=(D // bn, T // bm),
        in_specs=[pl.BlockSpec((bm, CONV_W), lambda j, i: (i, 0)),
                  pl.BlockSpec((bm, DN_W), lambda j, i: (i, 0)),
                  pl.BlockSpec((K, bn), lambda j, i: (0, j)),
                  pl.BlockSpec((bm, bn), lambda j, i: (i, j))],
        out_specs=pl.BlockSpec((bm, bn), lambda j, i: (i, j)),
        out_shape=jax.ShapeDtypeStruct((T, D), F32),
        compiler_params=_cparams(("arbitrary", "arbitrary")),
        name="out_proj",
    )(ya, yb, w_out, x)


def _router_kernel(h_ref, nw_ref, wr_ref, br_ref, xp_ref, idx_ref, gate_ref, rank_ref, cnt_ref,
                   carry_ref):
    bm, D = h_ref.shape

    @pl.when(pl.program_id(0) == 0)
    def _():
        carry_ref[...] = jnp.zeros(carry_ref.shape, F32)

    h = h_ref[...]
    ms = jnp.mean(h * h, axis=-1, keepdims=True)
    xn = h * lax.rsqrt(ms + EPS) * nw_ref[...]
    xp_ref[...] = pltpu.pack_elementwise([xn[:, :D // 2], xn[:, D // 2:]], packed_dtype=BF16)

    logits = _mm3_nt(xn, wr_ref[...]) + br_ref[...]
    lane_e = lax.broadcasted_iota(I32, logits.shape, 1)
    work = logits
    sels, vals, idxs = [], [], []
    for _ in range(TOP_K):
        m = jnp.max(work, axis=-1, keepdims=True)
        idx = jnp.min(jnp.where(work == m, lane_e, N_EXPERTS), axis=-1, keepdims=True)
        sel = lane_e == idx
        work = jnp.where(sel, -jnp.inf, work)
        sels.append(sel)
        vals.append(m)
        idxs.append(idx)
    exps = [jnp.exp(v - vals[0]) for v in vals]
    denom = exps[0] + exps[1] + exps[2] + exps[3]
    gates = [e / denom for e in exps]

    onehot = jnp.zeros(logits.shape, F32)
    for sel in sels:
        onehot = onehot + sel.astype(F32)
    ri = lax.broadcasted_iota(I32, (bm, bm), 0)
    ci = lax.broadcasted_iota(I32, (bm, bm), 1)
    tri = (ri > ci).astype(BF16)
    carry = carry_ref[...]
    before = jnp.dot(tri, onehot.astype(BF16), preferred_element_type=F32) + carry
    ranks = [jnp.sum(jnp.where(sel, before, 0.0), axis=-1, keepdims=True).astype(I32)
             for sel in sels]
    new_carry = carry + jnp.sum(onehot, axis=0, keepdims=True)
    carry_ref[...] = new_carry
    cnt_ref[...] = new_carry

    lane = lax.broadcasted_iota(I32, (bm, LANE), 1)

    def spread(cols):
        return jnp.where(lane == 0, cols[0],
                         jnp.where(lane == 1, cols[1], jnp.where(lane == 2, cols[2], cols[3])))

    idx_ref[...] = spread(idxs)
    gate_ref[...] = spread(gates)
    rank_ref[...] = spread(ranks)


def _router(h1, nw, w_router_t, b_router, bm=256):
    T, D = h1.shape
    return pl.pallas_call(
        _router_kernel,
        grid=(T // bm,),
        in_specs=[pl.BlockSpec((bm, D), lambda i: (i, 0)),
                  pl.BlockSpec((1, D), lambda i: (0, 0)),
                  pl.BlockSpec((N_EXPERTS, D), lambda i: (0, 0)),
                  pl.BlockSpec((1, N_EXPERTS), lambda i: (0, 0))],
        out_specs=[pl.BlockSpec((bm, D // 2), lambda i: (i, 0)),
                   pl.BlockSpec((bm, LANE), lambda i: (i, 0)),
                   pl.BlockSpec((bm, LANE), lambda i: (i, 0)),
                   pl.BlockSpec((bm, LANE), lambda i: (i, 0)),
                   pl.BlockSpec((1, N_EXPERTS), lambda i: (0, 0))],
        out_shape=[jax.ShapeDtypeStruct((T, D // 2), jnp.uint32),
                   jax.ShapeDtypeStruct((T, LANE), I32),
                   jax.ShapeDtypeStruct((T, LANE), F32),
                   jax.ShapeDtypeStruct((T, LANE), I32),
                   jax.ShapeDtypeStruct((1, N_EXPERTS), F32)],
        scratch_shapes=[pltpu.VMEM((1, N_EXPERTS), F32)],
        compiler_params=_cparams(("arbitrary",)),
        name="router",
    )(h1, nw, w_router_t, b_router)


def _dispatch_kernel(dest_ref, pad_start_ref, pad_len_ref, nused_ref, x_ref, xs_ref, zero_ref, sem):
    tb = x_ref.shape[0]
    i = pl.program_id(0)
    n_blocks = xs_ref.shape[0] // SLOT_BLK

    def row_copy(r, slot):
        return pltpu.make_async_copy(x_ref.at[pl.ds(r, 1)], xs_ref.at[pl.ds(slot, 1)], sem)

    def zero_row_copy(slot):
        return pltpu.make_async_copy(zero_ref.at[pl.ds(0, 1)], xs_ref.at[pl.ds(slot, 1)], sem)

    def zero_blk_copy(b):
        start = pl.multiple_of(b * SLOT_BLK, SLOT_BLK)
        return pltpu.make_async_copy(zero_ref, xs_ref.at[pl.ds(start, SLOT_BLK)], sem)

    @pl.when(i == 0)
    def _():
        zero_ref[...] = jnp.zeros(zero_ref.shape, zero_ref.dtype)

        def per_expert(e, c):
            start = pad_start_ref[e]
            n = pad_len_ref[e]
            lax.fori_loop(0, n, lambda j, cc: (zero_row_copy(start + j).start(), cc)[1], 0)
            lax.fori_loop(0, n, lambda j, cc: (zero_row_copy(start + j).wait(), cc)[1], 0)
            return c

        lax.fori_loop(0, N_EXPERTS, per_expert, 0)
        nu = nused_ref[0]
        lax.fori_loop(nu, n_blocks, lambda b, cc: (zero_blk_copy(b).start(), cc)[1], 0)
        lax.fori_loop(nu, n_blocks, lambda b, cc: (zero_blk_copy(b).wait(), cc)[1], 0)

    def issue(r, c):
        base = (i * tb + r) * TOP_K
        for k in range(TOP_K):
            row_copy(r, dest_ref[base + k]).start()
        return c

    def drain(r, c):
        for k in range(TOP_K):
            row_copy(0, 0).wait()
        return c

    lax.fori_loop(0, tb, issue, 0, unroll=DMA_ISSUE_UNROLL)
    lax.fori_loop(0, tb, drain, 0, unroll=DMA_ISSUE_UNROLL)


def _dispatch(dest_flat, pad_start, pad_len, nused, xp, n_slots, tb=256):
    T, W = xp.shape
    return pl.pallas_call(
        _dispatch_kernel,
        grid_spec=pltpu.PrefetchScalarGridSpec(
            num_scalar_prefetch=4,
            grid=(T // tb,),
            in_specs=[pl.BlockSpec((tb, W), lambda i, *_: (i, 0))],
            out_specs=pl.BlockSpec(memory_space=pl.ANY),
            scratch_shapes=[pltpu.VMEM((SLOT_BLK, W), xp.dtype),
                            pltpu.SemaphoreType.DMA(())],
        ),
        out_shape=jax.ShapeDtypeStruct((n_slots, W), xp.dtype),
        compiler_params=_cparams(("arbitrary",)),
        name="dispatch",
    )(dest_flat, pad_start, pad_len, nused, xp)


def _unpack_pair(xw):
    xa = pltpu.unpack_elementwise(xw, index=0, packed_dtype=BF16, unpacked_dtype=F32)
    xb = pltpu.unpack_elementwise(xw, index=1, packed_dtype=BF16, unpacked_dtype=F32)
    return xa.astype(BF16), xb.astype(BF16)


def _moe_kernel(item_e_ref, item_start_ref, item_nsub_ref, nused_ref,
                xs_ref, wgu_ref, wd_ref, bgu_ref, bd_ref, ys_ref,
                x_ref, h_ref, g_ref, gu_ref, d_ref, stage_ref, pend_ref,
                sem_x, sem_gu, sem_d, sem_o):
    i = pl.program_id(0)
    tf = gu_ref.shape[-1]
    tn = d_ref.shape[-1]
    half = gu_ref.shape[1] // 2
    nf = EXPERT_FF // tf
    nn = wd_ref.shape[-1] // tn
    n_blocks = ys_ref.shape[0] // MOE_SUB
    tp = tn // 2
    GATE, UP = 0, 1

    e = item_e_ref[i]
    start = pl.multiple_of(item_start_ref[i], MOE_SUB)
    nsub = item_nsub_ref[i]
    nxt_e = item_e_ref[i + 1]
    nxt_start = pl.multiple_of(item_start_ref[i + 1], MOE_SUB)
    nxt_nsub = item_nsub_ref[i + 1]

    gu_rows = gu_ref.shape[1] // MOE_DMA_SPLIT
    d_rows = d_ref.shape[1] // MOE_DMA_SPLIT

    def gu_copies(ee, part, f, slot=None):
        slot = part if slot is None else slot
        col0 = part * EXPERT_FF + pl.multiple_of(f * tf, tf)
        return [pltpu.make_async_copy(
            wgu_ref.at[0, ee, pl.ds(s * gu_rows, gu_rows), pl.ds(col0, tf)],
            gu_ref.at[slot, pl.ds(s * gu_rows, gu_rows)], sem_gu.at[slot])
            for s in range(MOE_DMA_SPLIT)]

    def d_copies(ee, n, slot):
        return [pltpu.make_async_copy(
            wd_ref.at[0, ee, pl.ds(s * d_rows, d_rows), pl.ds(pl.multiple_of(n * tn, tn), tn)],
            d_ref.at[slot, pl.ds(s * d_rows, d_rows)], sem_d.at[slot])
            for s in range(MOE_DMA_SPLIT)]

    def x_copy(row0, r):
        off = pl.multiple_of(r * MOE_SUB, MOE_SUB)
        return pltpu.make_async_copy(xs_ref.at[pl.ds(row0 + off, MOE_SUB)],
                                     x_ref.at[pl.ds(off, MOE_SUB)], sem_x)

    def out_copy(row0, nrows, n, slot):
        return pltpu.make_async_copy(
            stage_ref.at[slot, pl.ds(0, nrows)],
            ys_ref.at[pl.ds(row0, nrows), pl.ds(pl.multiple_of(n * tp, tp), tp)], sem_o.at[slot])

    def wait_out(slot):
        for units in (1, MOE_BLOCK_UNITS):
            @pl.when(pend_ref[slot] == units)
            def _(units=units):
                out_copy(start, units * MOE_SUB, 0, slot).wait()

    def for_row_blocks(body):
        nfull = nsub // MOE_BLOCK_UNITS
        full_rows = MOE_BLOCK_UNITS * MOE_SUB

        def full(p, c):
            body(p, pl.multiple_of(p * full_rows, full_rows), full_rows)
            return c

        def single(u, c):
            body(nfull + (u - nfull * MOE_BLOCK_UNITS), pl.multiple_of(u * MOE_SUB, MOE_SUB), MOE_SUB)
            return c

        lax.fori_loop(0, nfull, full, 0)
        lax.fori_loop(nfull * MOE_BLOCK_UNITS, nsub, single, 0)

    def start_gu(ee, part, f):
        for cp in gu_copies(ee, part, f):
            cp.start()

    def wait_gu(part):
        for cp in gu_copies(e, part, 0):
            cp.wait()

    def start_d(ee, n, slot):
        for cp in d_copies(ee, n, slot):
            cp.start()

    def load_rows(row0, count):
        lax.fori_loop(0, count, lambda r, c: (x_copy(row0, r).start(), c)[1], 0)

    @pl.when(i == 0)
    def _():
        pend_ref[0] = 0
        pend_ref[1] = 0
        stage_ref[0] = jnp.zeros(stage_ref.shape[1:], stage_ref.dtype)

        def zero_block(b, c):
            row0 = pl.multiple_of(b * MOE_SUB, MOE_SUB)
            for n in range(nn):
                out_copy(row0, MOE_SUB, n, 0).start()
            for n in range(nn):
                out_copy(row0, MOE_SUB, n, 0).wait()
            return c

        lax.fori_loop(nused_ref[0], n_blocks, zero_block, 0)

        @pl.when(nsub > 0)
        def _():
            load_rows(start, nsub)
            start_gu(e, GATE, 0)

    def x_dot(rows, slot):
        xa, xb = _unpack_pair(x_ref[rows, :])
        return (jnp.dot(xa, gu_ref[slot, :half].astype(BF16), preferred_element_type=F32)
                + jnp.dot(xb, gu_ref[slot, half:].astype(BF16), preferred_element_type=F32))

    def gate_up_tiles(f, c):
        wait_gu(GATE)
        start_gu(e, UP, f)
        bias_g = bgu_ref[e, pl.ds(f, 1), :]

        def gate_rows(blk, off, nrows):
            rows = pl.ds(off, nrows)
            g_ref[rows, :] = jnp.minimum(x_dot(rows, GATE) + bias_g, SWIGLU_LIMIT)

        for_row_blocks(gate_rows)

        wait_gu(UP)

        @pl.when(f + 1 < nf)
        def _():
            start_gu(e, GATE, f + 1)

        @pl.when(f + 1 == nf)
        def _():
            start_d(e, 0, 0)

        @pl.when(jnp.logical_and(f + 1 == nf, nxt_nsub > 0))
        def _():
            start_gu(nxt_e, GATE, 0)

        bias_u = bgu_ref[e, pl.ds(nf + f, 1), :]

        def up_rows(blk, off, nrows):
            rows = pl.ds(off, nrows)
            gate = g_ref[rows, :]
            up = jnp.clip(x_dot(rows, UP) + bias_u, -SWIGLU_LIMIT, SWIGLU_LIMIT)
            glu = gate * _sigmoid(SWIGLU_ALPHA * gate)
            h_ref[f, rows, :] = ((up + 1.0) * glu).astype(BF16)

        for_row_blocks(up_rows)
        return c

    def down_tiles(n, c):
        slot = n % 2
        for cp in d_copies(e, n, slot):
            cp.wait()

        @pl.when(n + 1 < nn)
        def _():
            start_d(e, n + 1, 1 - slot)

        @pl.when(jnp.logical_and(n == 0, nxt_nsub > 0))
        def _():
            load_rows(nxt_start, nxt_nsub)

        bias_d = bd_ref[e, pl.ds(n, 1), :]

        def down_rows(blk, off, nrows):
            os = blk % 2
            wait_out(os)
            rows = pl.ds(off, nrows)
            acc = bias_d
            for f in range(nf):
                acc = acc + jnp.dot(h_ref[f, rows, :], d_ref[slot, f * tf:(f + 1) * tf, :].astype(BF16),
                                    preferred_element_type=F32)
            stage_ref[os, pl.ds(0, nrows)] = pltpu.pack_elementwise(
                [acc[:, :tp], acc[:, tp:]], packed_dtype=BF16)
            out_copy(start + off, nrows, n, os).start()
            pend_ref[os] = nrows // MOE_SUB

        for_row_blocks(down_rows)
        return c

    @pl.when(nsub > 0)
    def _():
        lax.fori_loop(0, nsub, lambda r, c: (x_copy(start, r).wait(), c)[1], 0)
        lax.fori_loop(0, nf, gate_up_tiles, 0)
        lax.fori_loop(0, nn, down_tiles, 0)
        for os in range(2):
            wait_out(os)
            pend_ref[os] = 0


def _moe(item_e, item_start, item_nsub, nused, xs, w_gate_up, b_gate_up, w_down, b_down,
         tf=MOE_TF, tn=MOE_TN):
    n_slots, W = xs.shape
    D = 2 * W
    n_items = item_e.shape[0] - 1
    rb = MOE_ITEM_SUBS * MOE_SUB
    b_gate_up = b_gate_up.reshape(N_EXPERTS, 2 * EXPERT_FF // tf, tf)
    b_down = b_down.reshape(N_EXPERTS, D // tn, tn)
    any_spec = pl.BlockSpec(memory_space=pl.ANY)
    return pl.pallas_call(
        _moe_kernel,
        grid_spec=pltpu.PrefetchScalarGridSpec(
            num_scalar_prefetch=4,
            grid=(n_items,),
            in_specs=[any_spec, any_spec, any_spec,
                      pl.BlockSpec(b_gate_up.shape, lambda i, *_: (0, 0, 0)),
                      pl.BlockSpec(b_down.shape, lambda i, *_: (0, 0, 0))],
            out_specs=any_spec,
            scratch_shapes=[pltpu.VMEM((rb, W), xs.dtype),
                            pltpu.VMEM((EXPERT_FF // tf, rb, tf), BF16),
                            pltpu.VMEM((rb, tf), F32),
                            pltpu.VMEM((2, D, tf), F32),
                            pltpu.VMEM((2, EXPERT_FF, tn), F32),
                            pltpu.VMEM((2, MOE_BLOCK_UNITS * MOE_SUB, tn // 2), xs.dtype),
                            pltpu.SMEM((2,), I32),
                            pltpu.SemaphoreType.DMA(()),
                            pltpu.SemaphoreType.DMA((2,)),
                            pltpu.SemaphoreType.DMA((2,)),
                            pltpu.SemaphoreType.DMA((2,))],
        ),
        out_shape=jax.ShapeDtypeStruct((n_slots, D // 2), xs.dtype),
        compiler_params=_cparams(("arbitrary",)),
        name="moe",
    )(item_e, item_start, item_nsub, nused, xs, w_gate_up, w_down, b_gate_up, b_down)


def _combine_kernel(dest_ref, h_ref, gate_ref, ys_ref, nw_ref, o_ref, buf_ref, sem):
    tb = h_ref.shape[0]
    i = pl.program_id(0)
    n_steps = pl.num_programs(0)

    def row_copy(slot, buf, k, r):
        return pltpu.make_async_copy(ys_ref.at[pl.ds(slot, 1)], buf_ref.at[buf, k, pl.ds(r, 1)],
                                     sem.at[buf])

    def gather(step, buf):
        def issue(r, c):
            base = (step * tb + r) * TOP_K
            for k in range(TOP_K):
                row_copy(dest_ref[base + k], buf, k, r).start()
            return c

        lax.fori_loop(0, tb, issue, 0, unroll=DMA_ISSUE_UNROLL)

    @pl.when(i == 0)
    def _():
        gather(0, 0)

    cur = i % 2

    @pl.when(i + 1 < n_steps)
    def _():
        gather(i + 1, 1 - cur)

    def drain(r, c):
        for k in range(TOP_K):
            row_copy(0, cur, k, r).wait()
        return c

    lax.fori_loop(0, tb, drain, 0, unroll=DMA_ISSUE_UNROLL)

    gates = gate_ref[...]
    D = h_ref.shape[1]
    tp = MOE_TN // 2
    ssq = jnp.zeros((tb, 1), F32)
    for n in range(D // MOE_TN):
        cols = (slice(n * MOE_TN, n * MOE_TN + tp), slice(n * MOE_TN + tp, (n + 1) * MOE_TN))
        parts = [h_ref[:, c] for c in cols]
        for k in range(TOP_K):
            words = buf_ref[cur, k, :, n * tp:(n + 1) * tp]
            g = gates[:, k:k + 1]
            for half in range(2):
                y = pltpu.unpack_elementwise(words, index=half, packed_dtype=BF16,
                                             unpacked_dtype=F32)
                parts[half] = parts[half] + y * g
        for half in range(2):
            o_ref[:, cols[half]] = parts[half]
            ssq = ssq + jnp.sum(parts[half] * parts[half], axis=-1, keepdims=True)
    o_ref[...] = o_ref[...] * lax.rsqrt(ssq / D + EPS) * nw_ref[...]


def _combine(dest_flat, h1, gates, ys, nw, tb=128):
    T, D = h1.shape
    return pl.pallas_call(
        _combine_kernel,
        grid_spec=pltpu.PrefetchScalarGridSpec(
            num_scalar_prefetch=1,
            grid=(T // tb,),
            in_specs=[pl.BlockSpec((tb, D), lambda i, *_: (i, 0)),
                      pl.BlockSpec((tb, LANE), lambda i, *_: (i, 0)),
                      pl.BlockSpec(memory_space=pl.ANY),
                      pl.BlockSpec((1, D), lambda i, *_: (0, 0))],
            out_specs=pl.BlockSpec((tb, D), lambda i, *_: (i, 0)),
            scratch_shapes=[pltpu.VMEM((2, TOP_K, tb, D // 2), ys.dtype),
                            pltpu.SemaphoreType.DMA((2,))],
        ),
        out_shape=jax.ShapeDtypeStruct((T, D), F32),
        compiler_params=_cparams(("arbitrary",)),
        name="combine",
    )(dest_flat, h1, gates, ys, nw)


def _layer(h, norm1_w, w_in, conv_a_w, conv_a_norm_w, dn_conv_w, dn_A_log, dn_dt_bias, dn_norm_w,
           w_out, norm2_w, w_router, b_router, w_gate_up, b_gate_up, w_down, b_down, l):
    T, D = h.shape
    main_cols = 3 * CONV_W + 4 * DN_W
    w_in_t = w_in[l].T
    xn, gc, beta = _norm1(h, norm1_w[l][None], w_in_t, main_cols, dn_A_log[l][None],
                          dn_dt_bias[l][None])
    proj = _in_proj(xn, w_in_t, main_cols)
    ya = _conv_mix(proj, conv_a_w[l], conv_a_norm_w[l][None])
    gct = gc.T.reshape(N_HEADS, T // CHUNK, CHUNK)
    yb = _gdn(proj, gc, beta, gct, dn_conv_w[l], dn_norm_w[l][None])
    h1 = _out_proj(ya, yb, w_out[l], h)

    xp, idx, gates, rank, counts = _router(h1, norm2_w[l][None], w_router[l].T, b_router[l][None])
    top_idx = idx[:, :TOP_K]
    counts = counts[0].astype(I32)
    padded = ((counts + SLOT_BLK - 1) // SLOT_BLK) * SLOT_BLK
    pend = jnp.cumsum(padded)
    pstart = pend - padded
    dest_flat = (pstart[top_idx] + rank[:, :TOP_K]).reshape(-1)
    n_slots = T * TOP_K + N_EXPERTS * SLOT_BLK
    nb = n_slots // SLOT_BLK
    nused = (pend[-1] // SLOT_BLK).reshape(1)

    nsub_e = padded // SLOT_BLK
    items_e = (nsub_e + MOE_ITEM_SUBS - 1) // MOE_ITEM_SUBS
    item_end = jnp.cumsum(items_e)
    item_first = item_end - items_e
    n_items = N_EXPERTS + nb // MOE_ITEM_SUBS
    item = jnp.arange(n_items + 1, dtype=I32)
    item_e = jnp.minimum(jnp.sum((item_end[None, :] <= item[:, None]).astype(I32), axis=1),
                         N_EXPERTS - 1)
    item_j = item - item_first[item_e]
    item_valid = item < item_end[-1]
    item_start = jnp.where(item_valid, pstart[item_e] + item_j * (MOE_ITEM_SUBS * SLOT_BLK), 0)
    item_nsub = jnp.where(item_valid,
                          jnp.minimum(nsub_e[item_e] - item_j * MOE_ITEM_SUBS, MOE_ITEM_SUBS), 0)

    xs = _dispatch(dest_flat, pstart + counts, padded - counts, nused, xp, n_slots)
    ys = _moe(item_e, item_start.astype(I32), item_nsub.astype(I32), nused, xs,
              w_gate_up[l:l + 1], b_gate_up[l], w_down[l:l + 1], b_down[l])
    return h1, dest_flat, gates, ys


def kernel(x, norm1_w, w_in, conv_a_w, conv_a_norm_w, dn_conv_w, dn_A_log, dn_dt_bias, dn_norm_w,
           w_out, norm2_w, w_router, b_router, w_gate_up, b_gate_up, w_down, b_down, final_norm_w):
    B, S, D = x.shape
    depth = norm1_w.shape[0]
    assert depth == 1, "the final norm is fused into the last layer's combine"
    h = x.reshape(B * S, D)
    h1, dest_flat, gates, ys = _layer(
        h, norm1_w, w_in, conv_a_w, conv_a_norm_w, dn_conv_w, dn_A_log, dn_dt_bias, dn_norm_w,
        w_out, norm2_w, w_router, b_router, w_gate_up, b_gate_up, w_down, b_down, 0)
    out = _combine(dest_flat, h1, gates, ys, final_norm_w[None])
    return out.reshape(B, S, D)
```

```python
import functools

import jax
import jax.numpy as jnp
from jax import lax
from jax.experimental import pallas as pl
from jax.experimental.pallas import tpu as pltpu

F32 = jnp.float32
BF16 = jnp.bfloat16
I32 = jnp.int32

EPS = 1e-6
CHUNK = 64
HEAD_DIM = 128
N_HEADS = 16
CONV_W = 2048
DN_W = 2048
N_EXPERTS = 32
TOP_K = 4
EXPERT_FF = 2048
SWIGLU_LIMIT = 7.0
SWIGLU_ALPHA = 1.702

LANE = 128
VMEM_LIMIT = 56 * 1024 * 1024
SLOT_BLK = 128
MOE_SUB = SLOT_BLK
MOE_BLOCK_UNITS = 4
MOE_ITEM_SUBS = 10
MOE_TF = 512
MOE_TN = 512
MOE_DMA_SPLIT = 4
DMA_ISSUE_UNROLL = 4


def _cparams(sem):
    return pltpu.CompilerParams(dimension_semantics=sem, vmem_limit_bytes=VMEM_LIMIT)


def _mm(a, b):
    return jnp.dot(a.astype(BF16), b.astype(BF16), preferred_element_type=F32)


def _dot_nt(a, b):
    return lax.dot_general(a, b, (((1,), (1,)), ((), ())), preferred_element_type=F32)


def _mm_nt(a, b):
    return _dot_nt(a.astype(BF16), b.astype(BF16))


def _mm_tn(a, b):
    return lax.dot_general(a.astype(BF16), b.astype(BF16), (((0,), (0,)), ((), ())),
                           preferred_element_type=F32)


def _mm3_nt(x, wt):
    xh = x.astype(BF16)
    xl = (x - xh.astype(F32)).astype(BF16)
    wh = wt.astype(BF16)
    wl = (wt - wh.astype(F32)).astype(BF16)
    return _dot_nt(xh, wh) + _dot_nt(xl, wh) + _dot_nt(xh, wl)


def _sigmoid(x):
    return 1.0 / (1.0 + jnp.exp(-x))


def _norm1_kernel(x_ref, nw_ref, wab_ref, alog_ref, dtb_ref, xn_ref, gc_ref, beta_ref):
    x = x_ref[...]
    ms = jnp.mean(x * x, axis=-1, keepdims=True)
    xn = x * lax.rsqrt(ms + EPS) * nw_ref[...]
    xn_ref[...] = xn.astype(BF16)
    ab = _mm3_nt(xn, wab_ref[...])
    a = ab[:, :N_HEADS] + dtb_ref[...]
    b = ab[:, N_HEADS:]
    softplus = jnp.maximum(a, 0.0) + jnp.log1p(jnp.exp(-jnp.abs(a)))
    g = -jnp.exp(alog_ref[...]) * softplus
    pos = lax.broadcasted_iota(I32, g.shape, 0) % CHUNK
    shift = 1
    while shift < CHUNK:
        g = g + jnp.where(pos >= shift, pltpu.roll(g, shift, 0), 0.0)
        shift *= 2
    gc_ref[...] = g
    beta_ref[...] = _sigmoid(b)


def _norm1(x, nw, w_in_t, ab_row, alog, dtb, bm=256):
    T, D = x.shape
    nab = 2 * N_HEADS
    return pl.pallas_call(
        _norm1_kernel,
        grid=(T // bm,),
        in_specs=[pl.BlockSpec((bm, D), lambda i: (i, 0)),
                  pl.BlockSpec((1, D), lambda i: (0, 0)),
                  pl.BlockSpec((nab, D), lambda i: (ab_row // nab, 0)),
                  pl.BlockSpec((1, N_HEADS), lambda i: (0, 0)),
                  pl.BlockSpec((1, N_HEADS), lambda i: (0, 0))],
        out_specs=[pl.BlockSpec((bm, D), lambda i: (i, 0)),
                   pl.BlockSpec((bm, N_HEADS), lambda i: (i, 0)),
                   pl.BlockSpec((bm, N_HEADS), lambda i: (i, 0))],
        out_shape=[jax.ShapeDtypeStruct((T, D), BF16),
                   jax.ShapeDtypeStruct((T, N_HEADS), F32),
                   jax.ShapeDtypeStruct((T, N_HEADS), F32)],
        compiler_params=_cparams(("arbitrary",)),
        name="norm1",
    )(x, nw, w_in_t, alog, dtb)


def _in_proj_kernel(x_ref, w_ref, o_ref):
    o_ref[...] = _dot_nt(x_ref[...], w_ref[...].astype(BF16))


def _in_proj(xn, w_in_t, n_cols, bm=512, bn=1024):
    T, D = xn.shape
    return pl.pallas_call(
        _in_proj_kernel,
        grid=(n_cols // bn, T // bm),
        in_specs=[pl.BlockSpec((bm, D), lambda j, i: (i, 0)),
                  pl.BlockSpec((bn, D), lambda j, i: (j, 0))],
        out_specs=pl.BlockSpec((bm, bn), lambda j, i: (i, j)),
        out_shape=jax.ShapeDtypeStruct((T, n_cols), F32),
        compiler_params=_cparams(("arbitrary", "arbitrary")),
        name="in_proj",
    )(xn, w_in_t)


def _conv_mix_kernel(x_ref, b_ref, c_ref, xh_ref, ch_ref, cw_ref, nw_ref, o_ref):
    bm, bc = x_ref.shape
    u = c_ref[...] * x_ref[...]
    halo = jnp.where(pl.program_id(0) > 0, ch_ref[...] * xh_ref[...], 0.0)
    ext = jnp.concatenate([halo, u], axis=0)
    cw = cw_ref[...]
    conv = ext[6:6 + bm] * cw[0:1] + ext[7:7 + bm] * cw[1:2] + u * cw[2:3]
    y = b_ref[...] * conv
    nw = nw_ref[...]
    for g in range(bc // LANE):
        cs = slice(g * LANE, (g + 1) * LANE)
        yg = y[:, cs]
        ms = jnp.mean(yg * yg, axis=-1, keepdims=True)
        o_ref[:, cs] = (yg * lax.rsqrt(ms + EPS) * nw[:, cs]).astype(BF16)


def _conv_mix(proj, conv_w, norm_w, bm=512, bc=512):
    T = proj.shape[0]
    nc = CONV_W // bc
    hb = bm // 8
    halo_map = lambda off: (lambda i, j: (jnp.maximum(i * hb - 1, 0), j + off))
    return pl.pallas_call(
        _conv_mix_kernel,
        grid=(T // bm, nc),
        in_specs=[pl.BlockSpec((bm, bc), lambda i, j: (i, j)),
                  pl.BlockSpec((bm, bc), lambda i, j: (i, j + nc)),
                  pl.BlockSpec((bm, bc), lambda i, j: (i, j + 2 * nc)),
                  pl.BlockSpec((8, bc), halo_map(0)),
                  pl.BlockSpec((8, bc), halo_map(2 * nc)),
                  pl.BlockSpec((3, bc), lambda i, j: (0, j)),
                  pl.BlockSpec((1, bc), lambda i, j: (0, j))],
        out_specs=pl.BlockSpec((bm, bc), lambda i, j: (i, j)),
        out_shape=jax.ShapeDtypeStruct((T, CONV_W), BF16),
        compiler_params=_cparams(("arbitrary", "arbitrary")),
        name="conv_mix",
    )(proj, proj, proj, proj, proj, conv_w, norm_w)


def _gdn_kernel(q_ref, k_ref, v_ref, z_ref, gc_ref, beta_ref, gct_ref, cw_ref, nw_ref,
                y_ref, qs_ref, ks_ref, vs_ref, halo_ref, s_ref):
    tb = q_ref.shape[0]
    t = pl.program_id(0)

    @pl.when(t == 0)
    def _():
        halo_ref[...] = jnp.zeros(halo_ref.shape, F32)
        s_ref[...] = jnp.zeros(s_ref.shape, F32)

    for a, (src, dst) in enumerate(((q_ref, qs_ref), (k_ref, ks_ref), (v_ref, vs_ref))):
        for h in range(N_HEADS):
            cs = slice(h * HEAD_DIM, (h + 1) * HEAD_DIM)
            raw = src[:, cs]
            ext = jnp.concatenate([halo_ref[a, :, cs], raw], axis=0)
            cw = cw_ref[:, a * DN_W + h * HEAD_DIM:a * DN_W + (h + 1) * HEAD_DIM]
            c = (ext[5:5 + tb] * cw[0:1] + ext[6:6 + tb] * cw[1:2]
                 + ext[7:7 + tb] * cw[2:3] + raw * cw[3:4])
            c = c * _sigmoid(c)
            if a < 2:
                c = c * lax.rsqrt(jnp.sum(c * c, axis=-1, keepdims=True) + EPS)
            if a == 0:
                c = c * (HEAD_DIM ** -0.5)
            dst[:, cs] = c
            halo_ref[a, :, cs] = raw[tb - 8:tb]

    ii = lax.broadcasted_iota(I32, (CHUNK, CHUNK), 0)
    jj = lax.broadcasted_iota(I32, (CHUNK, CHUNK), 1)
    eye = (ii == jj).astype(F32)
    nw = nw_ref[...]

    heads = range(N_HEADS)
    cols = [slice(h * HEAD_DIM, (h + 1) * HEAD_DIM) for h in heads]

    def chunk(c, carry):
        rows = pl.ds(pl.multiple_of(c * CHUNK, CHUNK), CHUNK)
        cg = t * (tb // CHUNK) + c
        gcb = gc_ref[rows, :]
        btb = beta_ref[rows, :]
        q = [qs_ref[rows, cs] for cs in cols]
        k = [ks_ref[rows, cs] for cs in cols]
        v = [vs_ref[rows, cs] for cs in cols]
        z = [z_ref[rows, cs] for cs in cols]
        s = [s_ref[h] for h in heads]
        gcol = [gcb[:, h:h + 1] for h in heads]
        bcol = [btb[:, h:h + 1] for h in heads]
        grow = [gct_ref[h, pl.ds(cg, 1), :] for h in heads]
        gl = [g[:, CHUNK - 1:CHUNK] for g in grow]
        eg = [jnp.exp(g) for g in gcol]
        kb = [k[h] * bcol[h] for h in heads]
        a1 = [_mm_nt(jnp.concatenate([kb[h], q[h]], axis=0), k[h]) for h in heads]
        dec = [jnp.where(ii >= jj, jnp.exp(jnp.minimum(gcol[h] - grow[h], 0.0)), 0.0) for h in heads]
        lmat = [jnp.where(ii > jj, a1[h][:CHUNK] * dec[h], 0.0) for h in heads]
        qkd = [a1[h][CHUNK:] * dec[h] for h in heads]
        p = [eye - lmat[h] for h in heads]
        cur = [_mm(lmat[h], lmat[h]) for h in heads]
        power = 2
        while 2 * power < CHUNK:
            r = [_mm(jnp.concatenate([p[h], cur[h]], axis=0), cur[h]) for h in heads]
            p = [p[h] + r[h][:CHUNK] for h in heads]
            cur = [r[h][CHUNK:] for h in heads]
            power *= 2
        p = [p[h] + _mm(p[h], cur[h]) for h in heads]
        uw = [_mm(p[h], jnp.concatenate([v[h] * bcol[h], kb[h] * eg[h]], axis=1)) for h in heads]
        ws = [_mm(jnp.concatenate([uw[h][:, HEAD_DIM:], q[h] * eg[h]], axis=0), s[h]) for h in heads]
        vn = [uw[h][:, :HEAD_DIM] - ws[h][:CHUNK] for h in heads]
        o = [ws[h][CHUNK:] + _mm(qkd[h], vn[h]) for h in heads]
        kt = [k[h] * jnp.exp(gl[h] - gcol[h]) for h in heads]
        s_new = [s[h] * jnp.exp(gl[h]) + _mm_tn(kt[h], vn[h]) for h in heads]
        for h in heads:
            on = o[h] * lax.rsqrt(jnp.mean(o[h] * o[h], axis=-1, keepdims=True) + EPS) * nw
            y_ref[rows, cols[h]] = (on * (z[h] * _sigmoid(z[h]))).astype(BF16)
            s_ref[h] = s_new[h]
        return carry

    lax.fori_loop(0, tb // CHUNK, chunk, 0)


def _gdn(proj, gc, beta, gct, conv_w, norm_w, tb=256):
    T = proj.shape[0]
    qb = (3 * CONV_W) // DN_W
    n_chunks = T // CHUNK
    return pl.pallas_call(
        _gdn_kernel,
        grid=(T // tb,),
        in_specs=[pl.BlockSpec((tb, DN_W), lambda t: (t, qb)),
                  pl.BlockSpec((tb, DN_W), lambda t: (t, qb + 1)),
                  pl.BlockSpec((tb, DN_W), lambda t: (t, qb + 2)),
                  pl.BlockSpec((tb, DN_W), lambda t: (t, qb + 3)),
                  pl.BlockSpec((tb, N_HEADS), lambda t: (t, 0)),
                  pl.BlockSpec((tb, N_HEADS), lambda t: (t, 0)),
                  pl.BlockSpec((N_HEADS, n_chunks, CHUNK), lambda t: (0, 0, 0)),
                  pl.BlockSpec((4, 3 * DN_W), lambda t: (0, 0)),
                  pl.BlockSpec((1, HEAD_DIM), lambda t: (0, 0))],
        out_specs=pl.BlockSpec((tb, DN_W), lambda t: (t, 0)),
        out_shape=jax.ShapeDtypeStruct((T, DN_W), BF16),
        scratch_shapes=[pltpu.VMEM((tb, DN_W), F32),
                        pltpu.VMEM((tb, DN_W), F32),
                        pltpu.VMEM((tb, DN_W), F32),
                        pltpu.VMEM((3, 8, DN_W), F32),
                        pltpu.VMEM((N_HEADS, HEAD_DIM, HEAD_DIM), F32)],
        compiler_params=_cparams(("arbitrary",)),
        name="gdn",
    )(proj, proj, proj, proj, gc, beta, gct, conv_w, norm_w)


def _out_proj_kernel(ya_ref, yb_ref, w_ref, x_ref, o_ref):
    o_ref[...] = (x_ref[...]
                  + jnp.dot(ya_ref[...], w_ref[:CONV_W].astype(BF16), preferred_element_type=F32)
                  + jnp.dot(yb_ref[...], w_ref[CONV_W:].astype(BF16), preferred_element_type=F32))


def _out_proj(ya, yb, w_out, x, bm=512, bn=1024):
    T, D = x.shape
    K = w_out.shape[0]
    return pl.pallas_call(
        _out_proj_kernel,
        grid=(D // bn, T // bm),
        in_specs=[pl.BlockSpec((bm, CONV_W), lambda j, i: (i, 0)),
                  pl.BlockSpec((bm, DN_W), lambda j, i: (i, 0)),
                  pl.BlockSpec((K, bn), lambda j, i: (0, j)),
                  pl.BlockSpec((bm, bn), lambda j, i: (i, j))],
        out_specs=pl.BlockSpec((bm, bn), lambda j, i: (i, j)),
        out_shape=jax.ShapeDtypeStruct((T, D), F32),
        compiler_params=_cparams(("arbitrary", "arbitrary")),
        name="out_proj",
    )(ya, yb, w_out, x)


def _router_kernel(h_ref, nw_ref, wr_ref, br_ref, xp_ref, idx_ref, gate_ref, rank_ref, cnt_ref,
                   carry_ref):
    bm, D = h_ref.shape

    @pl.when(pl.program_id(0) == 0)
    def _():
        carry_ref[...] = jnp.zeros(carry_ref.shape, F32)

    h = h_ref[...]
    ms = jnp.mean(h * h, axis=-1, keepdims=True)
    xn = h * lax.rsqrt(ms + EPS) * nw_ref[...]
    xp_ref[...] = pltpu.pack_elementwise([xn[:, :D // 2], xn[:, D // 2:]], packed_dtype=BF16)

    logits = _mm3_nt(xn, wr_ref[...]) + br_ref[...]
    lane_e = lax.broadcasted_iota(I32, logits.shape, 1)
    work = logits
    sels, vals, idxs = [], [], []
    for _ in range(TOP_K):
        m = jnp.max(work, axis=-1, keepdims=True)
        idx = jnp.min(jnp.where(work == m, lane_e, N_EXPERTS), axis=-1, keepdims=True)
        sel = lane_e == idx
        work = jnp.where(sel, -jnp.inf, work)
        sels.append(sel)
        vals.append(m)
        idxs.append(idx)
    exps = [jnp.exp(v - vals[0]) for v in vals]
    denom = exps[0] + exps[1] + exps[2] + exps[3]
    gates = [e / denom for e in exps]

    onehot = jnp.zeros(logits.shape, F32)
    for sel in sels:
        onehot = onehot + sel.astype(F32)
    ri = lax.broadcasted_iota(I32, (bm, bm), 0)
    ci = lax.broadcasted_iota(I32, (bm, bm), 1)
    tri = (ri > ci).astype(BF16)
    carry = carry_ref[...]
    before = jnp.dot(tri, onehot.astype(BF16), preferred_element_type=F32) + carry
    ranks = [jnp.sum(jnp.where(sel, before, 0.0), axis=-1, keepdims=True).astype(I32)
             for sel in sels]
    new_carry = carry + jnp.sum(onehot, axis=0, keepdims=True)
    carry_ref[...] = new_carry
    cnt_ref[...] = new_carry

    lane = lax.broadcasted_iota(I32, (bm, LANE), 1)

    def spread(cols):
        return jnp.where(lane == 0, cols[0],
                         jnp.where(lane == 1, cols[1], jnp.where(lane == 2, cols[2], cols[3])))

    idx_ref[...] = spread(idxs)
    gate_ref[...] = spread(gates)
    rank_ref[...] = spread(ranks)


def _router(h1, nw, w_router_t, b_router, bm=256):
    T, D = h1.shape
    return pl.pallas_call(
        _router_kernel,
        grid=(T // bm,),
        in_specs=[pl.BlockSpec((bm, D), lambda i: (i, 0)),
                  pl.BlockSpec((1, D), lambda i: (0, 0)),
                  pl.BlockSpec((N_EXPERTS, D), lambda i: (0, 0)),
                  pl.BlockSpec((1, N_EXPERTS), lambda i: (0, 0))],
        out_specs=[pl.BlockSpec((bm, D // 2), lambda i: (i, 0)),
                   pl.BlockSpec((bm, LANE), lambda i: (i, 0)),
                   pl.BlockSpec((bm, LANE), lambda i: (i, 0)),
                   pl.BlockSpec((bm, LANE), lambda i: (i, 0)),
                   pl.BlockSpec((1, N_EXPERTS), lambda i: (0, 0))],
        out_shape=[jax.ShapeDtypeStruct((T, D // 2), jnp.uint32),
                   jax.ShapeDtypeStruct((T, LANE), I32),
                   jax.ShapeDtypeStruct((T, LANE), F32),
                   jax.ShapeDtypeStruct((T, LANE), I32),
                   jax.ShapeDtypeStruct((1, N_EXPERTS), F32)],
        scratch_shapes=[pltpu.VMEM((1, N_EXPERTS), F32)],
        compiler_params=_cparams(("arbitrary",)),
        name="router",
    )(h1, nw, w_router_t, b_router)


def _dispatch_kernel(dest_ref, pad_start_ref, pad_len_ref, nused_ref, x_ref, xs_ref, zero_ref, sem):
    tb = x_ref.shape[0]
    i = pl.program_id(0)
    n_blocks = xs_ref.shape[0] // SLOT_BLK

    def row_copy(r, slot):
        return pltpu.make_async_copy(x_ref.at[pl.ds(r, 1)], xs_ref.at[pl.ds(slot, 1)], sem)

    def zero_row_copy(slot):
        return pltpu.make_async_copy(zero_ref.at[pl.ds(0, 1)], xs_ref.at[pl.ds(slot, 1)], sem)

    def zero_blk_copy(b):
        start = pl.multiple_of(b * SLOT_BLK, SLOT_BLK)
        return pltpu.make_async_copy(zero_ref, xs_ref.at[pl.ds(start, SLOT_BLK)], sem)

    @pl.when(i == 0)
    def _():
        zero_ref[...] = jnp.zeros(zero_ref.shape, zero_ref.dtype)

        def per_expert(e, c):
            start = pad_start_ref[e]
            n = pad_len_ref[e]
            lax.fori_loop(0, n, lambda j, cc: (zero_row_copy(start + j).start(), cc)[1], 0)
            lax.fori_loop(0, n, lambda j, cc: (zero_row_copy(start + j).wait(), cc)[1], 0)
            return c

        lax.fori_loop(0, N_EXPERTS, per_expert, 0)
        nu = nused_ref[0]
        lax.fori_loop(nu, n_blocks, lambda b, cc: (zero_blk_copy(b).start(), cc)[1], 0)
        lax.fori_loop(nu, n_blocks, lambda b, cc: (zero_blk_copy(b).wait(), cc)[1], 0)

    def issue(r, c):
        base = (i * tb + r) * TOP_K
        for k in range(TOP_K):
            row_copy(r, dest_ref[base + k]).start()
        return c

    def drain(r, c):
        for k in range(TOP_K):
            row_copy(0, 0).wait()
        return c

    lax.fori_loop(0, tb, issue, 0, unroll=DMA_ISSUE_UNROLL)
    lax.fori_loop(0, tb, drain, 0, unroll=DMA_ISSUE_UNROLL)


def _dispatch(dest_flat, pad_start, pad_len, nused, xp, n_slots, tb=256):
    T, W = xp.shape
    return pl.pallas_call(
        _dispatch_kernel,
        grid_spec=pltpu.PrefetchScalarGridSpec(
            num_scalar_prefetch=4,
            grid=(T // tb,),
            in_specs=[pl.BlockSpec((tb, W), lambda i, *_: (i, 0))],
            out_specs=pl.BlockSpec(memory_space=pl.ANY),
            scratch_shapes=[pltpu.VMEM((SLOT_BLK, W), xp.dtype),
                            pltpu.SemaphoreType.DMA(())],
        ),
        out_shape=jax.ShapeDtypeStruct((n_slots, W), xp.dtype),
        compiler_params=_cparams(("arbitrary",)),
        name="dispatch",
    )(dest_flat, pad_start, pad_len, nused, xp)


def _unpack_pair(xw):
    xa = pltpu.unpack_elementwise(xw, index=0, packed_dtype=BF16, unpacked_dtype=F32)
    xb = pltpu.unpack_elementwise(xw, index=1, packed_dtype=BF16, unpacked_dtype=F32)
    return xa.astype(BF16), xb.astype(BF16)


def _moe_kernel(item_e_ref, item_start_ref, item_nsub_ref, nused_ref,
                xs_ref, wgu_ref, wd_ref, bgu_ref, bd_ref, ys_ref,
                x_ref, h_ref, g_ref, gu_ref, d_ref, stage_ref, pend_ref,
                sem_x, sem_gu, sem_d, sem_o):
    i = pl.program_id(0)
    tf = gu_ref.shape[-1]
    tn = d_ref.shape[-1]
    half = gu_ref.shape[1] // 2
    nf = EXPERT_FF // tf
    nn = wd_ref.shape[-1] // tn
    n_blocks = ys_ref.shape[0] // MOE_SUB
    tp = tn // 2
    GATE, UP = 0, 1

    e = item_e_ref[i]
    start = pl.multiple_of(item_start_ref[i], MOE_SUB)
    nsub = item_nsub_ref[i]
    nxt_e = item_e_ref[i + 1]
    nxt_start = pl.multiple_of(item_start_ref[i + 1], MOE_SUB)
    nxt_nsub = item_nsub_ref[i + 1]

    gu_rows = gu_ref.shape[1] // MOE_DMA_SPLIT
    d_rows = d_ref.shape[1] // MOE_DMA_SPLIT

    def gu_copies(ee, part, f, slot=None):
        slot = part if slot is None else slot
        col0 = part * EXPERT_FF + pl.multiple_of(f * tf, tf)
        return [pltpu.make_async_copy(
            wgu_ref.at[0, ee, pl.ds(s * gu_rows, gu_rows), pl.ds(col0, tf)],
            gu_ref.at[slot, pl.ds(s * gu_rows, gu_rows)], sem_gu.at[slot])
            for s in range(MOE_DMA_SPLIT)]

    def d_copies(ee, n, slot):
        return [pltpu.make_async_copy(
            wd_ref.at[0, ee, pl.ds(s * d_rows, d_rows), pl.ds(pl.multiple_of(n * tn, tn), tn)],
            d_ref.at[slot, pl.ds(s * d_rows, d_rows)], sem_d.at[slot])
            for s in range(MOE_DMA_SPLIT)]

    def x_copy(row0, r):
        off = pl.multiple_of(r * MOE_SUB, MOE_SUB)
        return pltpu.make_async_copy(xs_ref.at[pl.ds(row0 + off, MOE_SUB)],
                                     x_ref.at[pl.ds(off, MOE_SUB)], sem_x)

    def out_copy(row0, nrows, n, slot):
        return pltpu.make_async_copy(
            stage_ref.at[slot, pl.ds(0, nrows)],
            ys_ref.at[pl.ds(row0, nrows), pl.ds(pl.multiple_of(n * tp, tp), tp)], sem_o.at[slot])

    def wait_out(slot):
        for units in (1, 2, MOE_BLOCK_UNITS):
            @pl.when(pend_ref[slot] == units)
            def _(units=units):
                out_copy(start, units * MOE_SUB, 0, slot).wait()

    def for_row_blocks(body):
        nfull = nsub // MOE_BLOCK_UNITS
        rem = nsub % MOE_BLOCK_UNITS
        full_rows = MOE_BLOCK_UNITS * MOE_SUB

        def full(p, c):
            body(p, pl.multiple_of(p * full_rows, full_rows), full_rows)
            return c

        lax.fori_loop(0, nfull, full, 0)
        has2 = (rem // 2) % 2

        @pl.when(has2 == 1)
        def _():
            body(nfull, pl.multiple_of(nfull * full_rows, full_rows), 2 * MOE_SUB)

        @pl.when(rem % 2 == 1)
        def _():
            body(nfull + has2,
                 pl.multiple_of(nfull * full_rows + has2 * 2 * MOE_SUB, MOE_SUB), MOE_SUB)

    def start_gu(ee, part, f):
        for cp in gu_copies(ee, part, f):
            cp.start()

    def wait_gu(part):
        for cp in gu_copies(e, part, 0):
            cp.wait()

    def start_d(ee, n, slot):
        for cp in d_copies(ee, n, slot):
            cp.start()

    def load_rows(row0, count):
        lax.fori_loop(0, count, lambda r, c: (x_copy(row0, r).start(), c)[1], 0)

    @pl.when(i == 0)
    def _():
        pend_ref[0] = 0
        pend_ref[1] = 0
        stage_ref[0] = jnp.zeros(stage_ref.shape[1:], stage_ref.dtype)

        def zero_block(b, c):
            row0 = pl.multiple_of(b * MOE_SUB, MOE_SUB)
            for n in range(nn):
                out_copy(row0, MOE_SUB, n, 0).start()
            for n in range(nn):
                out_copy(row0, MOE_SUB, n, 0).wait()
            return c

        lax.fori_loop(nused_ref[0], n_blocks, zero_block, 0)

        @pl.when(nsub > 0)
        def _():
            load_rows(start, nsub)
            start_gu(e, GATE, 0)

    def x_dot(rows, slot):
        xa, xb = _unpack_pair(x_ref[rows, :])
        return (jnp.dot(xa, gu_ref[slot, :half].astype(BF16), preferred_element_type=F32)
                + jnp.dot(xb, gu_ref[slot, half:].astype(BF16), preferred_element_type=F32))

    def gate_up_tiles(f, c):
        wait_gu(GATE)
        start_gu(e, UP, f)
        bias_g = bgu_ref[e, pl.ds(f, 1), :]

        def gate_rows(blk, off, nrows):
            rows = pl.ds(off, nrows)
            g_ref[rows, :] = jnp.minimum(x_dot(rows, GATE) + bias_g, SWIGLU_LIMIT)

        for_row_blocks(gate_rows)

        wait_gu(UP)

        @pl.when(f + 1 < nf)
        def _():
            start_gu(e, GATE, f + 1)

        @pl.when(f + 1 == nf)
        def _():
            start_d(e, 0, 0)
            start_d(e, 1, 1)

        @pl.when(jnp.logical_and(f + 1 == nf, nxt_nsub > 0))
        def _():
            start_gu(nxt_e, GATE, 0)

        bias_u = bgu_ref[e, pl.ds(nf + f, 1), :]

        def up_rows(blk, off, nrows):
            rows = pl.ds(off, nrows)
            gate = g_ref[rows, :]
            up = jnp.clip(x_dot(rows, UP) + bias_u, -SWIGLU_LIMIT, SWIGLU_LIMIT)
            glu = gate * _sigmoid(SWIGLU_ALPHA * gate)
            h_ref[f, rows, :] = ((up + 1.0) * glu).astype(BF16)

        for_row_blocks(up_rows)
        return c

    def down_tiles(n, c):
        n_slots_d = d_ref.shape[0]
        slot = n % n_slots_d
        for cp in d_copies(e, n, slot):
            cp.wait()

        @pl.when(n + 2 < nn)
        def _():
            start_d(e, n + 2, (n + 2) % n_slots_d)

        @pl.when(jnp.logical_and(n == 0, nxt_nsub > 0))
        def _():
            load_rows(nxt_start, nxt_nsub)

        bias_d = bd_ref[e, pl.ds(n, 1), :]

        def down_rows(blk, off, nrows):
            os = blk % 2
            wait_out(os)
            rows = pl.ds(off, nrows)
            acc = bias_d
            for f in range(nf):
                acc = acc + jnp.dot(h_ref[f, rows, :], d_ref[slot, f * tf:(f + 1) * tf, :].astype(BF16),
                                    preferred_element_type=F32)
            stage_ref[os, pl.ds(0, nrows)] = pltpu.pack_elementwise(
                [acc[:, :tp], acc[:, tp:]], packed_dtype=BF16)
            out_copy(start + off, nrows, n, os).start()
            pend_ref[os] = nrows // MOE_SUB

        for_row_blocks(down_rows)
        return c

    @pl.when(nsub > 0)
    def _():
        lax.fori_loop(0, nsub, lambda r, c: (x_copy(start, r).wait(), c)[1], 0)
        lax.fori_loop(0, nf, gate_up_tiles, 0)
        lax.fori_loop(0, nn, down_tiles, 0)
        for os in range(2):
            wait_out(os)
            pend_ref[os] = 0


def _moe(item_e, item_start, item_nsub, nused, xs, w_gate_up, b_gate_up, w_down, b_down,
         tf=MOE_TF, tn=MOE_TN):
    n_slots, W = xs.shape
    D = 2 * W
    n_items = item_e.shape[0] - 1
    rb = MOE_ITEM_SUBS * MOE_SUB
    b_gate_up = b_gate_up.reshape(N_EXPERTS, 2 * EXPERT_FF // tf, tf)
    b_down = b_down.reshape(N_EXPERTS, D // tn, tn)
    any_spec = pl.BlockSpec(memory_space=pl.ANY)
    return pl.pallas_call(
        _moe_kernel,
        grid_spec=pltpu.PrefetchScalarGridSpec(
            num_scalar_prefetch=4,
            grid=(n_items,),
            in_specs=[any_spec, any_spec, any_spec,
                      pl.BlockSpec(b_gate_up.shape, lambda i, *_: (0, 0, 0)),
                      pl.BlockSpec(b_down.shape, lambda i, *_: (0, 0, 0))],
            out_specs=any_spec,
            scratch_shapes=[pltpu.VMEM((rb, W), xs.dtype),
                            pltpu.VMEM((EXPERT_FF // tf, rb, tf), BF16),
                            pltpu.VMEM((rb, tf), F32),
                            pltpu.VMEM((2, D, tf), F32),
                            pltpu.VMEM((3, EXPERT_FF, tn), F32),
                            pltpu.VMEM((2, MOE_BLOCK_UNITS * MOE_SUB, tn // 2), xs.dtype),
                            pltpu.SMEM((2,), I32),
                            pltpu.SemaphoreType.DMA(()),
                            pltpu.SemaphoreType.DMA((2,)),
                            pltpu.SemaphoreType.DMA((3,)),
                            pltpu.SemaphoreType.DMA((2,))],
        ),
        out_shape=jax.ShapeDtypeStruct((n_slots, D // 2), xs.dtype),
        compiler_params=_cparams(("arbitrary",)),
        name="moe",
    )(item_e, item_start, item_nsub, nused, xs, w_gate_up, w_down, b_gate_up, b_down)


def _combine_kernel(dest_ref, h_ref, gate_ref, ys_ref, nw_ref, o_ref, buf_ref, sem):
    tb = h_ref.shape[0]
    i = pl.program_id(0)
    n_steps = pl.num_programs(0)

    def row_copy(slot, buf, k, r):
        return pltpu.make_async_copy(ys_ref.at[pl.ds(slot, 1)], buf_ref.at[buf, k, pl.ds(r, 1)],
                                     sem.at[buf])

    def gather(step, buf):
        def issue(r, c):
            base = (step * tb + r) * TOP_K
            for k in range(TOP_K):
                row_copy(dest_ref[base + k], buf, k, r).start()
            return c

        lax.fori_loop(0, tb, issue, 0, unroll=DMA_ISSUE_UNROLL)

    @pl.when(i == 0)
    def _():
        gather(0, 0)

    cur = i % 2

    @pl.when(i + 1 < n_steps)
    def _():
        gather(i + 1, 1 - cur)

    def drain(r, c):
        for k in range(TOP_K):
            row_copy(0, cur, k, r).wait()
        return c

    lax.fori_loop(0, tb, drain, 0, unroll=DMA_ISSUE_UNROLL)

    gates = gate_ref[...]
    D = h_ref.shape[1]
    tp = MOE_TN // 2
    ssq = jnp.zeros((tb, 1), F32)
    for n in range(D // MOE_TN):
        cols = (slice(n * MOE_TN, n * MOE_TN + tp), slice(n * MOE_TN + tp, (n + 1) * MOE_TN))
        parts = [h_ref[:, c] for c in cols]
        for k in range(TOP_K):
            words = buf_ref[cur, k, :, n * tp:(n + 1) * tp]
            g = gates[:, k:k + 1]
            for half in range(2):
                y = pltpu.unpack_elementwise(words, index=half, packed_dtype=BF16,
                                             unpacked_dtype=F32)
                parts[half] = parts[half] + y * g
        for half in range(2):
            o_ref[:, cols[half]] = parts[half]
            ssq = ssq + jnp.sum(parts[half] * parts[half], axis=-1, keepdims=True)
    o_ref[...] = o_ref[...] * lax.rsqrt(ssq / D + EPS) * nw_ref[...]


def _combine(dest_flat, h1, gates, ys, nw, tb=128):
    T, D = h1.shape
    return pl.pallas_call(
        _combine_kernel,
        grid_spec=pltpu.PrefetchScalarGridSpec(
            num_scalar_prefetch=1,
            grid=(T // tb,),
            in_specs=[pl.BlockSpec((tb, D), lambda i, *_: (i, 0)),
                      pl.BlockSpec((tb, LANE), lambda i, *_: (i, 0)),
                      pl.BlockSpec(memory_space=pl.ANY),
                      pl.BlockSpec((1, D), lambda i, *_: (0, 0))],
            out_specs=pl.BlockSpec((tb, D), lambda i, *_: (i, 0)),
            scratch_shapes=[pltpu.VMEM((2, TOP_K, tb, D // 2), ys.dtype),
                            pltpu.SemaphoreType.DMA((2,))],
        ),
        out_shape=jax.ShapeDtypeStruct((T, D), F32),
        compiler_params=_cparams(("arbitrary",)),
        name="combine",
    )(dest_flat, h1, gates, ys, nw)


def _layer(h, norm1_w, w_in, conv_a_w, conv_a_norm_w, dn_conv_w, dn_A_log, dn_dt_bias, dn_norm_w,
           w_out, norm2_w, w_router, b_router, w_gate_up, b_gate_up, w_down, b_down, l):
    T, D = h.shape
    main_cols = 3 * CONV_W + 4 * DN_W
    w_in_t = w_in[l].T
    xn, gc, beta = _norm1(h, norm1_w[l][None], w_in_t, main_cols, dn_A_log[l][None],
                          dn_dt_bias[l][None])
    proj = _in_proj(xn, w_in_t, main_cols)
    ya = _conv_mix(proj, conv_a_w[l], conv_a_norm_w[l][None])
    gct = gc.T.reshape(N_HEADS, T // CHUNK, CHUNK)
    yb = _gdn(proj, gc, beta, gct, dn_conv_w[l], dn_norm_w[l][None])
    h1 = _out_proj(ya, yb, w_out[l], h)

    xp, idx, gates, rank, counts = _router(h1, norm2_w[l][None], w_router[l].T, b_router[l][None])
    top_idx = idx[:, :TOP_K]
    counts = counts[0].astype(I32)
    padded = ((counts + SLOT_BLK - 1) // SLOT_BLK) * SLOT_BLK
    pend = jnp.cumsum(padded)
    pstart = pend - padded
    dest_flat = (pstart[top_idx] + rank[:, :TOP_K]).reshape(-1)
    n_slots = T * TOP_K + N_EXPERTS * SLOT_BLK
    nb = n_slots // SLOT_BLK
    nused = (pend[-1] // SLOT_BLK).reshape(1)

    nsub_e = padded // SLOT_BLK
    items_e = (nsub_e + MOE_ITEM_SUBS - 1) // MOE_ITEM_SUBS
    item_end = jnp.cumsum(items_e)
    item_first = item_end - items_e
    n_items = N_EXPERTS + nb // MOE_ITEM_SUBS
    item = jnp.arange(n_items + 1, dtype=I32)
    item_e = jnp.minimum(jnp.sum((item_end[None, :] <= item[:, None]).astype(I32), axis=1),
                         N_EXPERTS - 1)
    item_j = item - item_first[item_e]
    item_valid = item < item_end[-1]
    item_start = jnp.where(item_valid, pstart[item_e] + item_j * (MOE_ITEM_SUBS * SLOT_BLK), 0)
    item_nsub = jnp.where(item_valid,
                          jnp.minimum(nsub_e[item_e] - item_j * MOE_ITEM_SUBS, MOE_ITEM_SUBS), 0)

    xs = _dispatch(dest_flat, pstart + counts, padded - counts, nused, xp, n_slots)
    ys = _moe(item_e, item_start.astype(I32), item_nsub.astype(I32), nused, xs,
              w_gate_up[l:l + 1], b_gate_up[l], w_down[l:l + 1], b_down[l])
    return h1, dest_flat, gates, ys


def kernel(x, norm1_w, w_in, conv_a_w, conv_a_norm_w, dn_conv_w, dn_A_log, dn_dt_bias, dn_norm_w,
           w_out, norm2_w, w_router, b_router, w_gate_up, b_gate_up, w_down, b_down, final_norm_w):
    B, S, D = x.shape
    depth = norm1_w.shape[0]
    assert depth == 1, "the final norm is fused into the last layer's combine"
    h = x.reshape(B * S, D)
    h1, dest_flat, gates, ys = _layer(
        h, norm1_w, w_in, conv_a_w, conv_a_norm_w, dn_conv_w, dn_A_log, dn_dt_bias, dn_norm_w,
        w_out, norm2_w, w_router, b_router, w_gate_up, b_gate_up, w_down, b_down, 0)
    out = _combine(dest_flat, h1, gates, ys, final_norm_w[None])
    return out.reshape(B, S, D)
```
